```python
import jax, jax.numpy as jnp
from jax import lax
import numpy as np

D_MODEL = 2048
BATCH = 4
SEQ = 8192
DEPTH = 1

POOL_WINDOWS = (2, 4, 8, 16)
POOL_GROUPS = 4
POOL_WIDTH = D_MODEL // 2
POOL_GROUP_DIM = POOL_WIDTH // POOL_GROUPS
GLA_HEADS = 4
GLA_KEY_DIM = D_MODEL // 2
GLA_VALUE_DIM = D_MODEL
GLA_HEAD_K = GLA_KEY_DIM // GLA_HEADS
GLA_HEAD_V = GLA_VALUE_DIM // GLA_HEADS
GLA_GATE_RANK = 16
GLA_GATE_TAU = 16.0
GLA_CHUNK = 64
D_FF = -(-8 * D_MODEL // (3 * 256)) * 256
N_BRANCHES = 2
EPS = 1e-6
IN_SIZES = (POOL_WIDTH, GLA_KEY_DIM, GLA_KEY_DIM, GLA_VALUE_DIM, GLA_GATE_RANK, GLA_VALUE_DIM, N_BRANCHES * D_MODEL)
D_IN = sum(IN_SIZES)

kernel_name = "hybrid_pool_gla_gated_block"


def rms_norm(x, g):
    xf = x.astype(jnp.float32)
    y = xf * lax.rsqrt(jnp.mean(xf * xf, axis=-1, keepdims=True) + EPS)
    return (y * g.astype(jnp.float32)).astype(x.dtype)


def split_combined(z):
    idx = [int(i) for i in np.cumsum(IN_SIZES)[:-1]]
    return jnp.split(z, idx, axis=-1)


def pool_mixer(p, w_pool, pool_scale):
    B, T, _ = p.shape
    pg = p.reshape(B, T, POOL_GROUPS, POOL_GROUP_DIM).astype(jnp.float32)
    cs = jnp.concatenate([jnp.zeros((B, 1, POOL_GROUPS, POOL_GROUP_DIM), jnp.float32),
                          jnp.cumsum(pg, axis=1)], axis=1)
    pos = jnp.arange(T)
    outs = []
    for g, w in enumerate(POOL_WINDOWS):
        c = cs[:, :, g]
        lo = jnp.maximum(pos + 1 - w, 0)
        win_sum = c[:, 1:] - jnp.take(c, lo, axis=1)
        count = (pos + 1 - lo).astype(jnp.float32)
        outs.append(win_sum / count[None, :, None] - pg[:, :, g])
    d = jnp.stack(outs, axis=2).astype(p.dtype)
    y = jnp.einsum('btgc,gcd->btgd', d, w_pool).reshape(B, T, POOL_WIDTH)
    return y * pool_scale


def gla_chunked(q, k, v, log_a):
    B, T, H, dk = q.shape
    dv = v.shape[-1]
    nC = T // GLA_CHUNK
    C = GLA_CHUNK

    def chunks(t):
        return t.astype(jnp.float32).reshape(B, nC, C, H, t.shape[-1]).transpose(1, 0, 3, 2, 4)

    qc = chunks(q) * (dk ** -0.5)
    kc, vc = chunks(k), chunks(v)
    Gc = jnp.cumsum(chunks(log_a), axis=3)
    mask = jnp.tril(jnp.ones((C, C), dtype=bool))

    def step(S, inp):
        qi, ki, vi, Gi = inp
        o_inter = jnp.einsum('bhik,bhkv->bhiv', qi * jnp.exp(Gi), S)
        diff = Gi[:, :, :, None, :] - Gi[:, :, None, :, :]
        decay = jnp.exp(jnp.where(mask[:, :, None], diff, -jnp.inf))
        A = jnp.einsum('bhik,bhjk,bhijk->bhij', qi, ki, decay)
        o_intra = jnp.einsum('bhij,bhjv->bhiv', A, vi)
        G_last = Gi[:, :, -1]
        k_dec = ki * jnp.exp(G_last[:, :, None] - Gi)
        S_new = jnp.exp(G_last)[..., None] * S + jnp.einsum('bhjk,bhjv->bhkv', k_dec, vi)
        return S_new, o_inter + o_intra

    S0 = jnp.zeros((B, H, dk, dv), jnp.float32)
    _, o = lax.scan(step, S0, (qc, kc, vc, Gc))
    return o.transpose(1, 0, 3, 2, 4).reshape(B, T, H, dv).astype(v.dtype)


def setup_inputs(seed: int = 0) -> dict:
    key = jax.random.key(seed)
    ks = jax.random.split(key, 20)
    f32 = jnp.float32
    L = DEPTH

    def nrm(k, shape, fan_in):
        return jax.random.normal(k, shape, f32) * (fan_in ** -0.5)

    def gain(k, shape):
        return 1.0 + 0.05 * jax.random.normal(k, shape, f32)

    return {
        "x": jax.random.normal(ks[0], (BATCH, SEQ, D_MODEL), f32),
        "norm_mix_pre": gain(ks[1], (L, D_MODEL)),
        "w_in": nrm(ks[2], (L, D_MODEL, D_IN), D_MODEL),
        "w_gate_up": nrm(ks[3], (L, GLA_GATE_RANK, GLA_KEY_DIM), GLA_GATE_RANK),
        "b_gate": 0.1 * jax.random.normal(ks[4], (L, GLA_KEY_DIM), f32),
        "w_pool": nrm(ks[5], (L, POOL_GROUPS, POOL_GROUP_DIM, POOL_GROUP_DIM), POOL_GROUP_DIM),
        "pool_scale": gain(ks[6], (L, POOL_WIDTH)),
        "gla_norm": gain(ks[7], (L, GLA_HEAD_V)),
        "w_branch_a": nrm(ks[8], (L, POOL_WIDTH, D_MODEL), POOL_WIDTH),
        "w_branch_b": nrm(ks[9], (L, GLA_VALUE_DIM, D_MODEL), GLA_VALUE_DIM),
        "b_branch_gates": 0.01 * jax.random.normal(ks[10], (L, N_BRANCHES, D_MODEL), f32),
        "w_out": nrm(ks[11], (L, D_MODEL, D_MODEL), D_MODEL),
        "norm_mix_post": gain(ks[12], (L, D_MODEL)),
        "norm_ffn_pre": gain(ks[13], (L, D_MODEL)),
        "w_ffn_gate": nrm(ks[14], (L, D_MODEL, D_FF), D_MODEL),
        "w_ffn_up": nrm(ks[15], (L, D_MODEL, D_FF), D_MODEL),
        "w_ffn_down": nrm(ks[16], (L, D_FF, D_MODEL), D_FF),
        "norm_ffn_post": gain(ks[17], (L, D_MODEL)),
    }


def reference(x, norm_mix_pre, w_in, w_gate_up, b_gate, w_pool, pool_scale, gla_norm,
              w_branch_a, w_branch_b, b_branch_gates, w_out, norm_mix_post,
              norm_ffn_pre, w_ffn_gate, w_ffn_up, w_ffn_down, norm_ffn_post):
    B, T, _ = x.shape
    for l in range(DEPTH):
        h = rms_norm(x, norm_mix_pre[l])
        z = h @ w_in[l]
        p, q, k, v, g_lr, r, gate_logits = split_combined(z)
        y_a = pool_mixer(p, w_pool[l], pool_scale[l]) @ w_branch_a[l]
        log_a = jax.nn.log_sigmoid((g_lr @ w_gate_up[l] + b_gate[l]).astype(jnp.float32)) / GLA_GATE_TAU
        o = gla_chunked(q.reshape(B, T, GLA_HEADS, GLA_HEAD_K),
                        k.reshape(B, T, GLA_HEADS, GLA_HEAD_K),
                        v.reshape(B, T, GLA_HEADS, GLA_HEAD_V),
                        log_a.reshape(B, T, GLA_HEADS, GLA_HEAD_K))
        o = rms_norm(o, gla_norm[l]).reshape(B, T, GLA_VALUE_DIM) * jax.nn.silu(r)
        y_b = o @ w_branch_b[l]
        gates = jax.nn.sigmoid(gate_logits.reshape(B, T, N_BRANCHES, D_MODEL) + b_branch_gates[l])
        mixed = (gates[:, :, 0] * y_a + gates[:, :, 1] * y_b) @ w_out[l]
        x = x + rms_norm(mixed, norm_mix_post[l])
        h = rms_norm(x, norm_ffn_pre[l])
        f = (jax.nn.silu(h @ w_ffn_gate[l]) * (h @ w_ffn_up[l])) @ w_ffn_down[l]
        x = x + rms_norm(f, norm_ffn_post[l])
    return x
```

```python
import functools

import jax
import jax.numpy as jnp
from jax import lax
from jax.experimental import pallas as pl
from jax.experimental.pallas import tpu as pltpu

F32 = jnp.float32
BF16 = jnp.bfloat16

EPS = 1e-6
POOL_WINDOWS = (2, 4, 8, 16)
POOL_HALO = 16
GLA_HEADS = 4
GLA_GATE_TAU = 16.0
GLA_CHUNK = 64
LANES = 128
VMEM_LIMIT = 56 * 1024 * 1024


def _params(sem):
    return pltpu.CompilerParams(dimension_semantics=sem, vmem_limit_bytes=VMEM_LIMIT)


def _rms(x, g):
    ms = jnp.mean(x * x, axis=-1, keepdims=True)
    return x * lax.rsqrt(ms + EPS) * g


def _dot(a, b):
    return jnp.dot(a, b, preferred_element_type=F32)


def _inproj_kernel(x_ref, g_ref, w_ref, wglr_ref, z_ref, glr_ref, h_ref):
    @pl.when(pl.program_id(1) == 0)
    def _():
        hb = _rms(x_ref[...], g_ref[...]).astype(BF16)
        h_ref[...] = hb
        glr_ref[...] = _dot(hb, wglr_ref[...])

    z_ref[...] = _dot(h_ref[...], w_ref[...]).astype(BF16)


def _inproj(x2, g, w_main, w_glr, tm, tn):
    n, d = x2.shape
    dz = w_main.shape[1]
    return pl.pallas_call(
        _inproj_kernel,
        grid=(n // tm, dz // tn),
        in_specs=[
            pl.BlockSpec((tm, d), lambda i, j: (i, 0)),
            pl.BlockSpec((1, d), lambda i, j: (0, 0)),
            pl.BlockSpec((d, tn), lambda i, j: (0, j)),
            pl.BlockSpec((d, LANES), lambda i, j: (0, 0)),
        ],
        out_specs=[
            pl.BlockSpec((tm, tn), lambda i, j: (i, j)),
            pl.BlockSpec((tm, LANES), lambda i, j: (i, 0)),
        ],
        out_shape=[
            jax.ShapeDtypeStruct((n, dz), BF16),
            jax.ShapeDtypeStruct((n, LANES), F32),
        ],
        scratch_shapes=[pltpu.VMEM((tm, d), BF16)],
        compiler_params=_params(("parallel", "arbitrary")),
        name="in_proj",
    )(x2, g, w_main, w_glr)


def _pool_kernel(p_ref, wp_ref, sc_ref, o_ref, carry_ref, *, tt):
    t = pl.program_id(1)

    @pl.when(t == 0)
    def _():
        carry_ref[...] = jnp.zeros_like(carry_ref)

    p = p_ref[...].astype(F32)
    ext = jnp.concatenate([carry_ref[...], p], axis=0)
    carry_ref[...] = p[tt - POOL_HALO:, :]
    pos = t * tt + lax.broadcasted_iota(jnp.int32, (tt, 1), 0)
    gd = wp_ref.shape[1]
    outs = []
    for g, w in enumerate(POOL_WINDOWS):
        s = ext[:, g * gd:(g + 1) * gd]
        shift = 1
        while shift < w:
            s = s + pltpu.roll(s, shift, 0)
            shift *= 2
        count = jnp.minimum(pos + 1, w).astype(F32)
        d = s[POOL_HALO:, :] / count - p[:, g * gd:(g + 1) * gd]
        y = _dot(d.astype(BF16), wp_ref[g])
        outs.append(y * sc_ref[:, g * gd:(g + 1) * gd])
    o_ref[...] = jnp.concatenate(outs, axis=1).astype(BF16)


def _pool(z, w_pool, scale, batch, seq, col_block, tt):
    width = w_pool.shape[0] * w_pool.shape[1]
    nt = seq // tt
    return pl.pallas_call(
        functools.partial(_pool_kernel, tt=tt),
        grid=(batch, nt),
        in_specs=[
            pl.BlockSpec((tt, width), lambda b, t: (b * nt + t, col_block)),
            pl.BlockSpec(w_pool.shape, lambda b, t: (0, 0, 0)),
            pl.BlockSpec((1, width), lambda b, t: (0, 0)),
        ],
        out_specs=pl.BlockSpec((tt, width), lambda b, t: (b * nt + t, 0)),
        out_shape=jax.ShapeDtypeStruct((batch * seq, width), BF16),
        scratch_shapes=[pltpu.VMEM((POOL_HALO, width), F32)],
        compiler_params=_params(("parallel", "arbitrary")),
        name="pool_mixer",
    )(z, w_pool, scale)


def _gla_kernel(q_ref, k_ref, v_ref, r_ref, glr_ref, wgu_ref, bg_ref, gn_ref,
                o_ref, s_ref, *, tt, scale):
    c = GLA_CHUNK
    half = c // 2

    @pl.when(pl.program_id(2) == 0)
    def _():
        s_ref[...] = jnp.zeros_like(s_ref)

    u = _dot(glr_ref[...].astype(BF16), wgu_ref[...]) + bg_ref[...]
    log_a = (jnp.minimum(u, 0.0) - jnp.log(1.0 + jnp.exp(-jnp.abs(u)))) * (1.0 / GLA_GATE_TAU)
    la_hi = log_a.astype(BF16)
    la_lo = (log_a - la_hi.astype(F32)).astype(BF16)

    row = lax.broadcasted_iota(jnp.int32, (c, c), 0)
    col = lax.broadcasted_iota(jnp.int32, (c, c), 1)
    causal = row >= col
    tri = causal.astype(BF16)
    ones = jnp.ones((c, LANES), BF16)
    dv = v_ref.shape[1]

    for ci in range(tt // c):
        rows = slice(ci * c, (ci + 1) * c)
        hi, lo = la_hi[rows], la_lo[rows]
        gcum = _dot(tri, hi) + _dot(tri, lo)
        tn_dims = (((0,), (0,)), ((), ()))
        glast_col = (lax.dot_general(hi, ones, tn_dims, preferred_element_type=F32)
                     + lax.dot_general(lo, ones, tn_dims, preferred_element_type=F32))
        g_mid = gcum[half - 1:half, :]
        g_last = gcum[c - 1:c, :]
        q = q_ref[rows, :].astype(F32) * scale
        k = k_ref[rows, :].astype(F32)
        v = v_ref[rows, :]
        qg = q * jnp.exp(gcum)
        qa = qg * jnp.exp(-g_mid)
        ka = k * jnp.exp(g_mid - gcum)
        kd = ka * jnp.exp(g_last - g_mid)
        a = lax.dot_general(qa.astype(BF16), ka.astype(BF16), (((1,), (1,)), ((), ())),
                            preferred_element_type=F32)
        a = jnp.where(causal, a, 0.0)
        s_old = s_ref[...]
        o = _dot(qg.astype(BF16), s_old.astype(BF16)) + _dot(a.astype(BF16), v)
        decay = jnp.exp(glast_col)
        decay = jnp.concatenate([decay] * (dv // LANES), axis=1)
        s_ref[...] = decay * s_old + lax.dot_general(kd.astype(BF16), v, tn_dims,
                                                     preferred_element_type=F32)
        r = r_ref[rows, :].astype(F32)
        o_ref[rows, :] = (_rms(o, gn_ref[...]) * (r * jax.nn.sigmoid(r))).astype(BF16)


def _gla(z, glr, w_gate_up, b_gate, gla_norm, batch, seq, cols, tt):
    dk = w_gate_up.shape[1] // GLA_HEADS
    dv = gla_norm.shape[1]
    nt = seq // tt
    q0, k0, v0, r0 = cols
    return pl.pallas_call(
        functools.partial(_gla_kernel, tt=tt, scale=dk ** -0.5),
        grid=(batch, GLA_HEADS, nt),
        in_specs=[
            pl.BlockSpec((tt, dk), lambda b, h, t: (b * nt + t, q0 // dk + h)),
            pl.BlockSpec((tt, dk), lambda b, h, t: (b * nt + t, k0 // dk + h)),
            pl.BlockSpec((tt, dv), lambda b, h, t: (b * nt + t, v0 // dv + h)),
            pl.BlockSpec((tt, dv), lambda b, h, t: (b * nt + t, r0 // dv + h)),
            pl.BlockSpec((tt, LANES), lambda b, h, t: (b * nt + t, 0)),
            pl.BlockSpec((LANES, dk), lambda b, h, t: (0, h)),
            pl.BlockSpec((1, dk), lambda b, h, t: (0, h)),
            pl.BlockSpec((1, dv), lambda b, h, t: (0, 0)),
        ],
        out_specs=pl.BlockSpec((tt, dv), lambda b, h, t: (b * nt + t, h)),
        out_shape=jax.ShapeDtypeStruct((batch * seq, GLA_HEADS * dv), BF16),
        scratch_shapes=[pltpu.VMEM((dk, dv), F32)],
        compiler_params=_params(("parallel", "parallel", "arbitrary")),
        name="gla",
    )(z, z, z, z, glr, w_gate_up, b_gate, gla_norm)


def _mix_kernel(pm_ref, on_ref, wa_ref, wb_ref, ga_ref, gb_ref, bias_ref, o_ref):
    y_a = _dot(pm_ref[...], wa_ref[...])
    y_b = _dot(on_ref[...], wb_ref[...])
    gate_a = jax.nn.sigmoid(ga_ref[...].astype(F32) + bias_ref[0:1, :])
    gate_b = jax.nn.sigmoid(gb_ref[...].astype(F32) + bias_ref[1:2, :])
    o_ref[...] = (gate_a * y_a + gate_b * y_b).astype(BF16)


def _mix(pm, on, w_a, w_b, z, bias, gate_col, tm, tn):
    n = pm.shape[0]
    d = w_a.shape[1]
    ga0 = gate_col // tn
    gb0 = (gate_col + d) // tn
    return pl.pallas_call(
        _mix_kernel,
        grid=(n // tm, d // tn),
        in_specs=[
            pl.BlockSpec((tm, pm.shape[1]), lambda i, j: (i, 0)),
            pl.BlockSpec((tm, on.shape[1]), lambda i, j: (i, 0)),
            pl.BlockSpec((w_a.shape[0], tn), lambda i, j: (0, j)),
            pl.BlockSpec((w_b.shape[0], tn), lambda i, j: (0, j)),
            pl.BlockSpec((tm, tn), lambda i, j: (i, ga0 + j)),
            pl.BlockSpec((tm, tn), lambda i, j: (i, gb0 + j)),
            pl.BlockSpec((2, tn), lambda i, j: (0, j)),
        ],
        out_specs=pl.BlockSpec((tm, tn), lambda i, j: (i, j)),
        out_shape=jax.ShapeDtypeStruct((n, d), BF16),
        compiler_params=_params(("parallel", "arbitrary")),
        name="branch_mix",
    )(pm, on, w_a, w_b, z, z, bias)


def _outproj_kernel(m_ref, w_ref, x_ref, g_ref, o_ref):
    y = _dot(m_ref[...], w_ref[...])
    o_ref[...] = x_ref[...] + _rms(y, g_ref[...])


def _outproj(mixed, w_out, x2, g, tm):
    n, d = x2.shape
    return pl.pallas_call(
        _outproj_kernel,
        grid=(n // tm,),
        in_specs=[
            pl.BlockSpec((tm, d), lambda i: (i, 0)),
            pl.BlockSpec((d, d), lambda i: (0, 0)),
            pl.BlockSpec((tm, d), lambda i: (i, 0)),
            pl.BlockSpec((1, d), lambda i: (0, 0)),
        ],
        out_specs=pl.BlockSpec((tm, d), lambda i: (i, 0)),
        out_shape=jax.ShapeDtypeStruct((n, d), F32),
        compiler_params=_params(("parallel",)),
        name="out_proj",
    )(mixed, w_out, x2, g)


def _ffn_kernel(x_ref, gpre_ref, wg_ref, wu_ref, wd_ref, gpost_ref, o_ref, h_ref, acc_ref):
    j = pl.program_id(1)

    @pl.when(j == 0)
    def _():
        h_ref[...] = _rms(x_ref[...], gpre_ref[...]).astype(BF16)
        acc_ref[...] = jnp.zeros_like(acc_ref)

    h = h_ref[...]
    gate = _dot(h, wg_ref[...])
    up = _dot(h, wu_ref[...])
    act = (gate * jax.nn.sigmoid(gate) * up).astype(BF16)
    acc_ref[...] += _dot(act, wd_ref[...])

    @pl.when(j == pl.num_programs(1) - 1)
    def _():
        o_ref[...] = x_ref[...] + _rms(acc_ref[...], gpost_ref[...])


def _ffn(x1, g_pre, w_gate, w_up, w_down, g_post, tm, tf):
    n, d = x1.shape
    dff = w_gate.shape[1]
    return pl.pallas_call(
        _ffn_kernel,
        grid=(n // tm, dff // tf),
        in_specs=[
            pl.BlockSpec((tm, d), lambda i, j: (i, 0)),
            pl.BlockSpec((1, d), lambda i, j: (0, 0)),
            pl.BlockSpec((d, tf), lambda i, j: (0, j)),
            pl.BlockSpec((d, tf), lambda i, j: (0, j)),
            pl.BlockSpec((tf, d), lambda i, j: (j, 0)),
            pl.BlockSpec((1, d), lambda i, j: (0, 0)),
        ],
        out_specs=pl.BlockSpec((tm, d), lambda i, j: (i, 0)),
        out_shape=jax.ShapeDtypeStruct((n, d), F32),
        scratch_shapes=[pltpu.VMEM((tm, d), BF16), pltpu.VMEM((tm, d), F32)],
        compiler_params=_params(("parallel", "arbitrary")),
        name="ffn",
    )(x1, g_pre, w_gate, w_up, w_down, g_post)


def _layer(x2, batch, seq, norm_mix_pre, w_in, w_gate_up, b_gate, w_pool, pool_scale, gla_norm,
           w_branch_a, w_branch_b, b_branch_gates, w_out, norm_mix_post,
           norm_ffn_pre, w_ffn_gate, w_ffn_up, w_ffn_down, norm_ffn_post):
    d = x2.shape[1]
    pool_w = w_branch_a.shape[0]
    key_w = w_gate_up.shape[1]
    val_w = w_branch_b.shape[0]
    rank = w_gate_up.shape[0]

    o_p, o_q, o_k = 0, pool_w, pool_w + key_w
    o_v = o_k + key_w
    o_g = o_v + val_w
    o_r = o_g + rank
    o_gate = o_r + val_w
    w_main = jnp.concatenate(
        [w_in[:, o_v:o_g], w_in[:, o_r:o_gate], w_in[:, o_gate:],
         w_in[:, o_p:o_q], w_in[:, o_q:o_k], w_in[:, o_k:o_v]], axis=1).astype(BF16)
    c_v, c_r, c_gate = 0, val_w, 2 * val_w
    c_p = c_gate + 2 * d
    c_q, c_k = c_p + pool_w, c_p + pool_w + key_w
    w_glr = jnp.pad(w_in[:, o_g:o_r], ((0, 0), (0, LANES - rank))).astype(BF16)
    w_gu = jnp.pad(w_gate_up, ((0, LANES - rank), (0, 0))).astype(BF16)

    z, glr = _inproj(x2, norm_mix_pre[None, :], w_main, w_glr, tm=1024, tn=1024)
    pm = _pool(z, w_pool.astype(BF16), pool_scale[None, :], batch, seq, c_p // pool_w, tt=512)
    on = _gla(z, glr, w_gu, b_gate[None, :], gla_norm[None, :], batch, seq,
              (c_q, c_k, c_v, c_r), tt=256)
    mixed = _mix(pm, on, w_branch_a.astype(BF16), w_branch_b.astype(BF16), z,
                 b_branch_gates, c_gate, tm=1024, tn=1024)
    x1 = _outproj(mixed, w_out.astype(BF16), x2, norm_mix_post[None, :], tm=512)
    return _ffn(x1, norm_ffn_pre[None, :], w_ffn_gate.astype(BF16), w_ffn_up.astype(BF16),
                w_ffn_down.astype(BF16), norm_ffn_post[None, :], tm=512, tf=512)


def kernel(x, norm_mix_pre, w_in, w_gate_up, b_gate, w_pool, pool_scale, gla_norm, w_branch_a,
           w_branch_b, b_branch_gates, w_out, norm_mix_post, norm_ffn_pre, w_ffn_gate, w_ffn_up,
           w_ffn_down, norm_ffn_post):
    batch, seq, d = x.shape
    x2 = x.reshape(batch * seq, d)
    params = (norm_mix_pre, w_in, w_gate_up, b_gate, w_pool, pool_scale, gla_norm, w_branch_a,
              w_branch_b, b_branch_gates, w_out, norm_mix_post, norm_ffn_pre, w_ffn_gate,
              w_ffn_up, w_ffn_down, norm_ffn_post)
    for layer in range(norm_mix_pre.shape[0]):
        x2 = _layer(x2, batch, seq, *(p[layer] for p in params))
    return x2.reshape(batch, seq, d)
```

```python
import functools

import jax
import jax.numpy as jnp
from jax import lax
from jax.experimental import pallas as pl
from jax.experimental.pallas import tpu as pltpu

F32 = jnp.float32
BF16 = jnp.bfloat16

EPS = 1e-6
POOL_WINDOWS = (2, 4, 8, 16)
POOL_HALO = 16
GLA_HEADS = 4
GLA_GATE_TAU = 16.0
GLA_CHUNK = 128
LOG2E = 1.4426950408889634
SUBLANES = 8
LANES = 128
VMEM_LIMIT = 56 * 1024 * 1024


def _params(sem):
    return pltpu.CompilerParams(dimension_semantics=sem, vmem_limit_bytes=VMEM_LIMIT)


def _rms(x, g):
    ms = jnp.mean(x * x, axis=-1, keepdims=True)
    return x * lax.rsqrt(ms + EPS) * g


def _dot(a, b):
    return jnp.dot(a, b, preferred_element_type=F32)


def _inproj_kernel(x_ref, g_ref, w_ref, wglr_ref, z_ref, glr_ref, h_ref):
    @pl.when(pl.program_id(1) == 0)
    def _():
        hb = _rms(x_ref[...], g_ref[...]).astype(BF16)
        h_ref[...] = hb
        glr_ref[...] = _dot(hb, wglr_ref[...])

    z_ref[...] = _dot(h_ref[...], w_ref[...]).astype(BF16)


def _inproj(x2, g, w_main, w_glr, tm, tn):
    n, d = x2.shape
    dz = w_main.shape[1]
    return pl.pallas_call(
        _inproj_kernel,
        grid=(n // tm, dz // tn),
        in_specs=[
            pl.BlockSpec((tm, d), lambda i, j: (i, 0)),
            pl.BlockSpec((1, d), lambda i, j: (0, 0)),
            pl.BlockSpec((d, tn), lambda i, j: (0, j)),
            pl.BlockSpec((d, LANES), lambda i, j: (0, 0)),
        ],
        out_specs=[
            pl.BlockSpec((tm, tn), lambda i, j: (i, j)),
            pl.BlockSpec((tm, LANES), lambda i, j: (i, 0)),
        ],
        out_shape=[
            jax.ShapeDtypeStruct((n, dz), BF16),
            jax.ShapeDtypeStruct((n, LANES), F32),
        ],
        scratch_shapes=[pltpu.VMEM((tm, d), BF16)],
        compiler_params=_params(("parallel", "arbitrary")),
        name="in_proj",
    )(x2, g, w_main, w_glr)


def _pool_kernel(p_ref, wp_ref, sc_ref, o_ref, carry_ref, *, tt):
    t = pl.program_id(1)

    @pl.when(t == 0)
    def _():
        carry_ref[...] = jnp.zeros_like(carry_ref)

    p = p_ref[...].astype(F32)
    ext = jnp.concatenate([carry_ref[...], p], axis=0)
    carry_ref[...] = p[tt - POOL_HALO:, :]
    pos = t * tt + lax.broadcasted_iota(jnp.int32, (tt, 1), 0)
    gd = wp_ref.shape[1]
    outs = []
    for g, w in enumerate(POOL_WINDOWS):
        s = ext[:, g * gd:(g + 1) * gd]
        shift = 1
        while shift < w:
            s = s + pltpu.roll(s, shift, 0)
            shift *= 2
        count = jnp.minimum(pos + 1, w).astype(F32)
        d = s[POOL_HALO:, :] / count - p[:, g * gd:(g + 1) * gd]
        y = _dot(d.astype(BF16), wp_ref[g])
        outs.append(y * sc_ref[:, g * gd:(g + 1) * gd])
    o_ref[...] = jnp.concatenate(outs, axis=1).astype(BF16)


def _pool(z, w_pool, scale, batch, seq, col_block, tt):
    width = w_pool.shape[0] * w_pool.shape[1]
    nt = seq // tt
    return pl.pallas_call(
        functools.partial(_pool_kernel, tt=tt),
        grid=(batch, nt),
        in_specs=[
            pl.BlockSpec((tt, width), lambda b, t: (b * nt + t, col_block)),
            pl.BlockSpec(w_pool.shape, lambda b, t: (0, 0, 0)),
            pl.BlockSpec((1, width), lambda b, t: (0, 0)),
        ],
        out_specs=pl.BlockSpec((tt, width), lambda b, t: (b * nt + t, 0)),
        out_shape=jax.ShapeDtypeStruct((batch * seq, width), BF16),
        scratch_shapes=[pltpu.VMEM((POOL_HALO, width), F32)],
        compiler_params=_params(("parallel", "arbitrary")),
        name="pool_mixer",
    )(z, w_pool, scale)


_NT = (((1,), (1,)), ((), ()))
_TN = (((0,), (0,)), ((), ()))


def _store_tiles(ref, x):
    for j in range(ref.shape[0]):
        ref[j] = x[:, j * LANES:(j + 1) * LANES]


def _rows(ref, r0, n):
    return jnp.concatenate([ref[j, pl.ds(r0, n), :] for j in range(ref.shape[0])], axis=1)


def _bcast_row(ref, r, n):
    return jnp.concatenate([ref[j, pl.ds(r, n, stride=0), :] for j in range(ref.shape[0])], axis=1)


def _gla_kernel(q_ref, k_ref, v_ref, r_ref, glr_ref, wgu_ref, bg_ref, gn_ref,
                o_ref, s_ref, g_scr, q_scr, k_scr, *, tt, scale):
    c = GLA_CHUNK
    kw = q_ref.shape[1]
    dk = kw // GLA_HEADS
    dv = v_ref.shape[1] // GLA_HEADS
    heads = range(GLA_HEADS)

    @pl.when(pl.program_id(1) == 0)
    def _():
        s_ref[...] = jnp.zeros_like(s_ref)

    row = lax.broadcasted_iota(jnp.int32, (c, c), 0)
    col = lax.broadcasted_iota(jnp.int32, (c, c), 1)
    tri = (row >= col).astype(BF16)
    sel_r = lax.broadcasted_iota(jnp.int32, (LANES, c), 0)
    sel_c = lax.broadcasted_iota(jnp.int32, (LANES, c), 1)
    sel = (sel_r == (sel_c % SUBLANES)).astype(BF16)
    lane = lax.broadcasted_iota(jnp.int32, (SUBLANES, LANES), 1)
    sub = lax.broadcasted_iota(jnp.int32, (SUBLANES, LANES), 0)
    slot_mask = [(lane == s) & (sub >= s) for s in range(SUBLANES)]
    in_group = ((row // SUBLANES) == (col // SUBLANES)) & (row >= col)
    level_sizes = []
    b = c // 2
    while b >= 2 * SUBLANES:
        level_sizes.append(b)
        b //= 2
    same_block = [(row // b) == (col // b) for b in level_sizes]

    for ci in range(tt // c):
        rows = slice(ci * c, (ci + 1) * c)
        u = _dot(glr_ref[rows, :].astype(BF16), wgu_ref[...]) + bg_ref[...]
        la = (jnp.minimum(u, 0.0) - jnp.log(1.0 + jnp.exp(-jnp.abs(u)))) * (LOG2E / GLA_GATE_TAU)
        la_hi = la.astype(BF16)
        la_lo = (la - la_hi.astype(F32)).astype(BF16)
        gcum = _dot(tri, la_hi) + _dot(tri, la_lo)
        _store_tiles(g_scr, gcum)
        _store_tiles(q_scr, q_ref[rows, :].astype(F32) * scale)
        _store_tiles(k_scr, k_ref[rows, :].astype(F32))

        def level(b):
            half = b // 2
            qs, ks = [], []
            for s in range(0, c, b):
                gref = _bcast_row(g_scr, s + half, half)
                ks.append((_rows(k_scr, s, half) * jnp.exp2(gref - _rows(g_scr, s, half))).astype(BF16))
                ks.append(jnp.zeros((half, kw), BF16))
                qs.append(jnp.zeros((half, kw), BF16))
                qs.append((_rows(q_scr, s + half, half)
                           * jnp.exp2(_rows(g_scr, s + half, half) - gref)).astype(BF16))
            qb, kb = jnp.concatenate(qs, 0), jnp.concatenate(ks, 0)
            return [lax.dot_general(qb[:, h * dk:(h + 1) * dk], kb[:, h * dk:(h + 1) * dk], _NT,
                                    preferred_element_type=F32) for h in heads]

        a = level(c)
        for b, mask in zip(level_sizes, same_block):
            a = [jnp.where(mask, new, old) for new, old in zip(level(b), a)]

        groups = [[] for _ in heads]
        for g0 in range(0, c, SUBLANES):
            qg, gg = _rows(q_scr, g0, SUBLANES), _rows(g_scr, g0, SUBLANES)
            slots = [jnp.zeros((SUBLANES, LANES), F32) for _ in heads]
            for s in range(SUBLANES):
                p = qg * _bcast_row(k_scr, g0 + s, SUBLANES) * jnp.exp2(gg - _bcast_row(g_scr, g0 + s, SUBLANES))
                for h in heads:
                    score = jnp.sum(p[:, h * dk:(h + 1) * dk], axis=1, keepdims=True)
                    slots[h] = jnp.where(slot_mask[s], score, slots[h])
            for h in heads:
                groups[h].append(slots[h])
        a = [jnp.where(in_group, _dot(jnp.concatenate(groups[h], 0).astype(BF16), sel), a[h]).astype(BF16)
             for h in heads]

        gcum = _rows(g_scr, 0, c)
        g_last8 = _bcast_row(g_scr, c - 1, SUBLANES)
        g_last = jnp.concatenate([g_last8] * (c // SUBLANES), 0)
        qg = (_rows(q_scr, 0, c) * jnp.exp2(gcum)).astype(BF16)
        kd = (_rows(k_scr, 0, c) * jnp.exp2(g_last - gcum)).astype(BF16)
        decay_t = jnp.concatenate([jnp.exp2(g_last8)] * (LANES // SUBLANES), 0)
        for h in heads:
            kc = slice(h * dk, (h + 1) * dk)
            vc = slice(h * dv, (h + 1) * dv)
            v = v_ref[rows, vc]
            s_old = s_ref[h]
            o = _dot(qg[:, kc], s_old.astype(BF16)) + _dot(a[h], v)
            decay = decay_t[:, kc].T
            decay = jnp.concatenate([decay] * (dv // LANES), axis=1)
            s_ref[h] = decay * s_old + lax.dot_general(kd[:, kc], v, _TN, preferred_element_type=F32)
            r = r_ref[rows, vc].astype(F32)
            o_ref[rows, vc] = (_rms(o, gn_ref[...]) * (r * jax.nn.sigmoid(r))).astype(BF16)


def _gla(z, glr, w_gate_up, b_gate, gla_norm, batch, seq, cols, tt):
    key_w = w_gate_up.shape[1]
    dk = key_w // GLA_HEADS
    dv = gla_norm.shape[1]
    val_w = dv * GLA_HEADS
    nt = seq // tt
    q0, k0, v0, r0 = cols
    tiles = key_w // LANES
    return pl.pallas_call(
        functools.partial(_gla_kernel, tt=tt, scale=dk ** -0.5),
        grid=(batch, nt),
        in_specs=[
            pl.BlockSpec((tt, key_w), lambda b, t: (b * nt + t, q0 // key_w)),
            pl.BlockSpec((tt, key_w), lambda b, t: (b * nt + t, k0 // key_w)),
            pl.BlockSpec((tt, val_w), lambda b, t: (b * nt + t, v0 // val_w)),
            pl.BlockSpec((tt, val_w), lambda b, t: (b * nt + t, r0 // val_w)),
            pl.BlockSpec((tt, LANES), lambda b, t: (b * nt + t, 0)),
            pl.BlockSpec((LANES, key_w), lambda b, t: (0, 0)),
            pl.BlockSpec((1, key_w), lambda b, t: (0, 0)),
            pl.BlockSpec((1, dv), lambda b, t: (0, 0)),
        ],
        out_specs=pl.BlockSpec((tt, val_w), lambda b, t: (b * nt + t, 0)),
        out_shape=jax.ShapeDtypeStruct((batch * seq, val_w), BF16),
        scratch_shapes=[
            pltpu.VMEM((GLA_HEADS, dk, dv), F32),
            pltpu.VMEM((tiles, GLA_CHUNK, LANES), F32),
            pltpu.VMEM((tiles, GLA_CHUNK, LANES), F32),
            pltpu.VMEM((tiles, GLA_CHUNK, LANES), F32),
        ],
        compiler_params=_params(("parallel", "arbitrary")),
        name="gla",
    )(z, z, z, z, glr, w_gate_up, b_gate, gla_norm)


def _mix_kernel(pm_ref, on_ref, wa_ref, wb_ref, ga_ref, gb_ref, bias_ref, o_ref):
    y_a = _dot(pm_ref[...], wa_ref[...])
    y_b = _dot(on_ref[...], wb_ref[...])
    gate_a = jax.nn.sigmoid(ga_ref[...].astype(F32) + bias_ref[0:1, :])
    gate_b = jax.nn.sigmoid(gb_ref[...].astype(F32) + bias_ref[1:2, :])
    o_ref[...] = (gate_a * y_a + gate_b * y_b).astype(BF16)


def _mix(pm, on, w_a, w_b, z, bias, gate_col, tm, tn):
    n = pm.shape[0]
    d = w_a.shape[1]
    ga0 = gate_col // tn
    gb0 = (gate_col + d) // tn
    return pl.pallas_call(
        _mix_kernel,
        grid=(n // tm, d // tn),
        in_specs=[
            pl.BlockSpec((tm, pm.shape[1]), lambda i, j: (i, 0)),
            pl.BlockSpec((tm, on.shape[1]), lambda i, j: (i, 0)),
            pl.BlockSpec((w_a.shape[0], tn), lambda i, j: (0, j)),
            pl.BlockSpec((w_b.shape[0], tn), lambda i, j: (0, j)),
            pl.BlockSpec((tm, tn), lambda i, j: (i, ga0 + j)),
            pl.BlockSpec((tm, tn), lambda i, j: (i, gb0 + j)),
            pl.BlockSpec((2, tn), lambda i, j: (0, j)),
        ],
        out_specs=pl.BlockSpec((tm, tn), lambda i, j: (i, j)),
        out_shape=jax.ShapeDtypeStruct((n, d), BF16),
        compiler_params=_params(("parallel", "arbitrary")),
        name="branch_mix",
    )(pm, on, w_a, w_b, z, z, bias)


def _outproj_kernel(m_ref, w_ref, x_ref, g_ref, o_ref):
    y = _dot(m_ref[...], w_ref[...])
    o_ref[...] = x_ref[...] + _rms(y, g_ref[...])


def _outproj(mixed, w_out, x2, g, tm):
    n, d = x2.shape
    return pl.pallas_call(
        _outproj_kernel,
        grid=(n // tm,),
        in_specs=[
            pl.BlockSpec((tm, d), lambda i: (i, 0)),
            pl.BlockSpec((d, d), lambda i: (0, 0)),
            pl.BlockSpec((tm, d), lambda i: (i, 0)),
            pl.BlockSpec((1, d), lambda i: (0, 0)),
        ],
        out_specs=pl.BlockSpec((tm, d), lambda i: (i, 0)),
        out_shape=jax.ShapeDtypeStruct((n, d), F32),
        compiler_params=_params(("parallel",)),
        name="out_proj",
    )(mixed, w_out, x2, g)


def _ffn_kernel(x_ref, gpre_ref, wg_ref, wu_ref, wd_ref, gpost_ref, o_ref, h_ref, acc_ref):
    j = pl.program_id(1)

    @pl.when(j == 0)
    def _():
        h_ref[...] = _rms(x_ref[...], gpre_ref[...]).astype(BF16)
        acc_ref[...] = jnp.zeros_like(acc_ref)

    h = h_ref[...]
    gate = _dot(h, wg_ref[...])
    up = _dot(h, wu_ref[...])
    act = (gate * jax.nn.sigmoid(gate) * up).astype(BF16)
    acc_ref[...] += _dot(act, wd_ref[...])

    @pl.when(j == pl.num_programs(1) - 1)
    def _():
        o_ref[...] = x_ref[...] + _rms(acc_ref[...], gpost_ref[...])


def _ffn(x1, g_pre, w_gate, w_up, w_down, g_post, tm, tf):
    n, d = x1.shape
    dff = w_gate.shape[1]
    return pl.pallas_call(
        _ffn_kernel,
        grid=(n // tm, dff // tf),
        in_specs=[
            pl.BlockSpec((tm, d), lambda i, j: (i, 0)),
            pl.BlockSpec((1, d), lambda i, j: (0, 0)),
            pl.BlockSpec((d, tf), lambda i, j: (0, j)),
            pl.BlockSpec((d, tf), lambda i, j: (0, j)),
            pl.BlockSpec((tf, d), lambda i, j: (j, 0)),
            pl.BlockSpec((1, d), lambda i, j: (0, 0)),
        ],
        out_specs=pl.BlockSpec((tm, d), lambda i, j: (i, 0)),
        out_shape=jax.ShapeDtypeStruct((n, d), F32),
        scratch_shapes=[pltpu.VMEM((tm, d), BF16), pltpu.VMEM((tm, d), F32)],
        compiler_params=_params(("parallel", "arbitrary")),
        name="ffn",
    )(x1, g_pre, w_gate, w_up, w_down, g_post)


def _layer(x2, batch, seq, norm_mix_pre, w_in, w_gate_up, b_gate, w_pool, pool_scale, gla_norm,
           w_branch_a, w_branch_b, b_branch_gates, w_out, norm_mix_post,
           norm_ffn_pre, w_ffn_gate, w_ffn_up, w_ffn_down, norm_ffn_post):
    d = x2.shape[1]
    pool_w = w_branch_a.shape[0]
    key_w = w_gate_up.shape[1]
    val_w = w_branch_b.shape[0]
    rank = w_gate_up.shape[0]

    o_p, o_q, o_k = 0, pool_w, pool_w + key_w
    o_v = o_k + key_w
    o_g = o_v + val_w
    o_r = o_g + rank
    o_gate = o_r + val_w
    w_main = jnp.concatenate(
        [w_in[:, o_v:o_g], w_in[:, o_r:o_gate], w_in[:, o_gate:],
         w_in[:, o_p:o_q], w_in[:, o_q:o_k], w_in[:, o_k:o_v]], axis=1).astype(BF16)
    c_v, c_r, c_gate = 0, val_w, 2 * val_w
    c_p = c_gate + 2 * d
    c_q, c_k = c_p + pool_w, c_p + pool_w + key_w
    w_glr = jnp.pad(w_in[:, o_g:o_r], ((0, 0), (0, LANES - rank))).astype(BF16)
    w_gu = jnp.pad(w_gate_up, ((0, LANES - rank), (0, 0))).astype(BF16)

    z, glr = _inproj(x2, norm_mix_pre[None, :], w_main, w_glr, tm=1024, tn=1024)
    pm = _pool(z, w_pool.astype(BF16), pool_scale[None, :], batch, seq, c_p // pool_w, tt=512)
    on = _gla(z, glr, w_gu, b_gate[None, :], gla_norm[None, :], batch, seq,
              (c_q, c_k, c_v, c_r), tt=256)
    mixed = _mix(pm, on, w_branch_a.astype(BF16), w_branch_b.astype(BF16), z,
                 b_branch_gates, c_gate, tm=1024, tn=1024)
    x1 = _outproj(mixed, w_out.astype(BF16), x2, norm_mix_post[None, :], tm=512)
    return _ffn(x1, norm_ffn_pre[None, :], w_ffn_gate.astype(BF16), w_ffn_up.astype(BF16),
                w_ffn_down.astype(BF16), norm_ffn_post[None, :], tm=512, tf=512)


def kernel(x, norm_mix_pre, w_in, w_gate_up, b_gate, w_pool, pool_scale, gla_norm, w_branch_a,
           w_branch_b, b_branch_gates, w_out, norm_mix_post, norm_ffn_pre, w_ffn_gate, w_ffn_up,
           w_ffn_down, norm_ffn_post):
    batch, seq, d = x.shape
    x2 = x.reshape(batch * seq, d)
    params = (norm_mix_pre, w_in, w_gate_up, b_gate, w_pool, pool_scale, gla_norm, w_branch_a,
              w_branch_b, b_branch_gates, w_out, norm_mix_post, norm_ffn_pre, w_ffn_gate,
              w_ffn_up, w_ffn_down, norm_ffn_post)
    for layer in range(norm_mix_pre.shape[0]):
        x2 = _layer(x2, batch, seq, *(p[layer] for p in params))
    return x2.reshape(batch, seq, d)
```

```python
import functools

import jax
import jax.numpy as jnp
from jax import lax
from jax.experimental import pallas as pl
from jax.experimental.pallas import tpu as pltpu

F32 = jnp.float32
BF16 = jnp.bfloat16

EPS = 1e-6
POOL_WINDOWS = (2, 4, 8, 16)
POOL_HALO = 16
GLA_HEADS = 4
GLA_GATE_TAU = 16.0
GLA_CHUNK = 128
LOG2E = 1.4426950408889634
SUBLANES = 8
LANES = 128
VMEM_LIMIT = 56 * 1024 * 1024


def _params(sem):
    return pltpu.CompilerParams(dimension_semantics=sem, vmem_limit_bytes=VMEM_LIMIT)


def _rms(x, g):
    ms = jnp.mean(x * x, axis=-1, keepdims=True)
    return x * lax.rsqrt(ms + EPS) * g


def _dot(a, b):
    return jnp.dot(a, b, preferred_element_type=F32)


def _inproj_kernel(x_ref, g_ref, w_ref, wglr_ref, z_ref, glr_ref, h_ref):
    @pl.when(pl.program_id(1) == 0)
    def _():
        hb = _rms(x_ref[...], g_ref[...]).astype(BF16)
        h_ref[...] = hb
        glr_ref[...] = _dot(hb, wglr_ref[...])

    z_ref[...] = _dot(h_ref[...], w_ref[...]).astype(BF16)


def _inproj(x2, g, w_main, w_glr, tm, tn):
    n, d = x2.shape
    dz = w_main.shape[1]
    return pl.pallas_call(
        _inproj_kernel,
        grid=(n // tm, dz // tn),
        in_specs=[
            pl.BlockSpec((tm, d), lambda i, j: (i, 0)),
            pl.BlockSpec((1, d), lambda i, j: (0, 0)),
            pl.BlockSpec((d, tn), lambda i, j: (0, j)),
            pl.BlockSpec((d, LANES), lambda i, j: (0, 0)),
        ],
        out_specs=[
            pl.BlockSpec((tm, tn), lambda i, j: (i, j)),
            pl.BlockSpec((tm, LANES), lambda i, j: (i, 0)),
        ],
        out_shape=[
            jax.ShapeDtypeStruct((n, dz), BF16),
            jax.ShapeDtypeStruct((n, LANES), F32),
        ],
        scratch_shapes=[pltpu.VMEM((tm, d), BF16)],
        compiler_params=_params(("parallel", "arbitrary")),
        name="in_proj",
    )(x2, g, w_main, w_glr)


def _pool_kernel(p_ref, wp_ref, sc_ref, o_ref, carry_ref, *, tt):
    t = pl.program_id(1)

    @pl.when(t == 0)
    def _():
        carry_ref[...] = jnp.zeros_like(carry_ref)

    p = p_ref[...].astype(F32)
    ext = jnp.concatenate([carry_ref[...], p], axis=0)
    carry_ref[...] = p[tt - POOL_HALO:, :]
    pos = t * tt + lax.broadcasted_iota(jnp.int32, (tt, 1), 0)
    gd = wp_ref.shape[1]
    outs = []
    for g, w in enumerate(POOL_WINDOWS):
        s = ext[:, g * gd:(g + 1) * gd]
        shift = 1
        while shift < w:
            s = s + pltpu.roll(s, shift, 0)
            shift *= 2
        count = jnp.minimum(pos + 1, w).astype(F32)
        d = s[POOL_HALO:, :] / count - p[:, g * gd:(g + 1) * gd]
        y = _dot(d.astype(BF16), wp_ref[g])
        outs.append(y * sc_ref[:, g * gd:(g + 1) * gd])
    o_ref[...] = jnp.concatenate(outs, axis=1).astype(BF16)


def _pool(z, w_pool, scale, batch, seq, col_block, tt):
    width = w_pool.shape[0] * w_pool.shape[1]
    nt = seq // tt
    return pl.pallas_call(
        functools.partial(_pool_kernel, tt=tt),
        grid=(batch, nt),
        in_specs=[
            pl.BlockSpec((tt, width), lambda b, t: (b * nt + t, col_block)),
            pl.BlockSpec(w_pool.shape, lambda b, t: (0, 0, 0)),
            pl.BlockSpec((1, width), lambda b, t: (0, 0)),
        ],
        out_specs=pl.BlockSpec((tt, width), lambda b, t: (b * nt + t, 0)),
        out_shape=jax.ShapeDtypeStruct((batch * seq, width), BF16),
        scratch_shapes=[pltpu.VMEM((POOL_HALO, width), F32)],
        compiler_params=_params(("parallel", "arbitrary")),
        name="pool_mixer",
    )(z, w_pool, scale)


_NT = (((1,), (1,)), ((), ()))
_TN = (((0,), (0,)), ((), ()))


def _store_tiles(ref, x):
    for j in range(ref.shape[0]):
        ref[j] = x[:, j * LANES:(j + 1) * LANES]


def _rows(ref, r0, n):
    return jnp.concatenate([ref[j, pl.ds(r0, n), :] for j in range(ref.shape[0])], axis=1)


def _bcast_row(ref, r, n):
    return jnp.concatenate([ref[j, pl.ds(r, n, stride=0), :] for j in range(ref.shape[0])], axis=1)


def _gla_kernel(q_ref, k_ref, v_ref, glr_ref, wgu_ref, bg_ref,
                o_ref, s_ref, g_scr, q_scr, k_scr, *, tt, scale):
    c = GLA_CHUNK
    kw = q_ref.shape[1]
    dk = kw // GLA_HEADS
    dv = v_ref.shape[1] // GLA_HEADS
    heads = range(GLA_HEADS)

    @pl.when(pl.program_id(1) == 0)
    def _():
        s_ref[...] = jnp.zeros_like(s_ref)

    row = lax.broadcasted_iota(jnp.int32, (c, c), 0)
    col = lax.broadcasted_iota(jnp.int32, (c, c), 1)
    tri = (row >= col).astype(BF16)
    sel_r = lax.broadcasted_iota(jnp.int32, (LANES, c), 0)
    sel_c = lax.broadcasted_iota(jnp.int32, (LANES, c), 1)
    sel = (sel_r == (sel_c % SUBLANES)).astype(BF16)
    lane = lax.broadcasted_iota(jnp.int32, (SUBLANES, LANES), 1)
    sub = lax.broadcasted_iota(jnp.int32, (SUBLANES, LANES), 0)
    slot_mask = [(lane == s) & (sub >= s) for s in range(SUBLANES)]
    in_group = ((row // SUBLANES) == (col // SUBLANES)) & (row >= col)
    level_sizes = []
    b = c // 2
    while b >= 2 * SUBLANES:
        level_sizes.append(b)
        b //= 2
    same_block = [(row // b) == (col // b) for b in level_sizes]

    for ci in range(tt // c):
        rows = slice(ci * c, (ci + 1) * c)
        u = _dot(glr_ref[rows, :].astype(BF16), wgu_ref[...]) + bg_ref[...]
        la = (jnp.minimum(u, 0.0) - jnp.log(1.0 + jnp.exp(-jnp.abs(u)))) * (LOG2E / GLA_GATE_TAU)
        la_hi = la.astype(BF16)
        la_lo = (la - la_hi.astype(F32)).astype(BF16)
        gcum = _dot(tri, la_hi) + _dot(tri, la_lo)
        _store_tiles(g_scr, gcum)
        _store_tiles(q_scr, q_ref[rows, :].astype(F32) * scale)
        _store_tiles(k_scr, k_ref[rows, :].astype(F32))

        def level(b):
            half = b // 2
            qs, ks = [], []
            for s in range(0, c, b):
                gref = _bcast_row(g_scr, s + half, half)
                ks.append((_rows(k_scr, s, half) * jnp.exp2(gref - _rows(g_scr, s, half))).astype(BF16))
                ks.append(jnp.zeros((half, kw), BF16))
                qs.append(jnp.zeros((half, kw), BF16))
                qs.append((_rows(q_scr, s + half, half)
                           * jnp.exp2(_rows(g_scr, s + half, half) - gref)).astype(BF16))
            qb, kb = jnp.concatenate(qs, 0), jnp.concatenate(ks, 0)
            return [lax.dot_general(qb[:, h * dk:(h + 1) * dk], kb[:, h * dk:(h + 1) * dk], _NT,
                                    preferred_element_type=F32) for h in heads]

        a = level(c)
        for b, mask in zip(level_sizes, same_block):
            a = [jnp.where(mask, new, old) for new, old in zip(level(b), a)]

        groups = [[] for _ in heads]
        for g0 in range(0, c, SUBLANES):
            qg, gg = _rows(q_scr, g0, SUBLANES), _rows(g_scr, g0, SUBLANES)
            slots = [jnp.zeros((SUBLANES, LANES), F32) for _ in heads]
            for s in range(SUBLANES):
                p = qg * _bcast_row(k_scr, g0 + s, SUBLANES) * jnp.exp2(gg - _bcast_row(g_scr, g0 + s, SUBLANES))
                for h in heads:
                    score = jnp.sum(p[:, h * dk:(h + 1) * dk], axis=1, keepdims=True)
                    slots[h] = jnp.where(slot_mask[s], score, slots[h])
            for h in heads:
                groups[h].append(slots[h])
        a = [jnp.where(in_group, _dot(jnp.concatenate(groups[h], 0).astype(BF16), sel), a[h]).astype(BF16)
             for h in heads]

        gcum = _rows(g_scr, 0, c)
        g_last8 = _bcast_row(g_scr, c - 1, SUBLANES)
        g_last = jnp.concatenate([g_last8] * (c // SUBLANES), 0)
        qg = (_rows(q_scr, 0, c) * jnp.exp2(gcum)).astype(BF16)
        kd = (_rows(k_scr, 0, c) * jnp.exp2(g_last - gcum)).astype(BF16)
        decay_t = jnp.concatenate([jnp.exp2(g_last8)] * (LANES // SUBLANES), 0)
        for h in heads:
            kc = slice(h * dk, (h + 1) * dk)
            vc = slice(h * dv, (h + 1) * dv)
            v = v_ref[rows, vc]
            s_old = s_ref[h]
            o = _dot(qg[:, kc], s_old.astype(BF16)) + _dot(a[h], v)
            decay = decay_t[:, kc].T
            decay = jnp.concatenate([decay] * (dv // LANES), axis=1)
            s_ref[h] = decay * s_old + lax.dot_general(kd[:, kc], v, _TN, preferred_element_type=F32)
            o_ref[rows, vc] = o.astype(BF16)


def _gla(z, glr, w_gate_up, b_gate, val_w, batch, seq, cols, tt):
    key_w = w_gate_up.shape[1]
    dk = key_w // GLA_HEADS
    dv = val_w // GLA_HEADS
    nt = seq // tt
    q0, k0, v0 = cols
    tiles = key_w // LANES
    return pl.pallas_call(
        functools.partial(_gla_kernel, tt=tt, scale=dk ** -0.5),
        grid=(batch, nt),
        in_specs=[
            pl.BlockSpec((tt, key_w), lambda b, t: (b * nt + t, q0 // key_w)),
            pl.BlockSpec((tt, key_w), lambda b, t: (b * nt + t, k0 // key_w)),
            pl.BlockSpec((tt, val_w), lambda b, t: (b * nt + t, v0 // val_w)),
            pl.BlockSpec((tt, LANES), lambda b, t: (b * nt + t, 0)),
            pl.BlockSpec((LANES, key_w), lambda b, t: (0, 0)),
            pl.BlockSpec((1, key_w), lambda b, t: (0, 0)),
        ],
        out_specs=pl.BlockSpec((tt, val_w), lambda b, t: (b * nt + t, 0)),
        out_shape=jax.ShapeDtypeStruct((batch * seq, val_w), BF16),
        scratch_shapes=[
            pltpu.VMEM((GLA_HEADS, dk, dv), F32),
            pltpu.VMEM((tiles, GLA_CHUNK, LANES), F32),
            pltpu.VMEM((tiles, GLA_CHUNK, LANES), F32),
            pltpu.VMEM((tiles, GLA_CHUNK, LANES), F32),
        ],
        compiler_params=_params(("parallel", "arbitrary")),
        name="gla",
    )(z, z, z, glr, w_gate_up, b_gate)


def _mix_kernel(pm_ref, o_ref, r_ref, gn_ref, wa_ref, wb_ref, ga_ref, gb_ref, bias_ref,
                out_ref):
    dv = gn_ref.shape[1]
    y_b = None
    for h in range(o_ref.shape[1] // dv):
        vc = slice(h * dv, (h + 1) * dv)
        r = r_ref[:, vc].astype(F32)
        on = (_rms(o_ref[:, vc].astype(F32), gn_ref[...]) * (r * jax.nn.sigmoid(r))).astype(BF16)
        part = _dot(on, wb_ref[vc, :])
        y_b = part if y_b is None else y_b + part
    y_a = _dot(pm_ref[...], wa_ref[...])
    gate_a = jax.nn.sigmoid(ga_ref[...].astype(F32) + bias_ref[0:1, :])
    gate_b = jax.nn.sigmoid(gb_ref[...].astype(F32) + bias_ref[1:2, :])
    out_ref[...] = (gate_a * y_a + gate_b * y_b).astype(BF16)


def _mix(pm, o, z, gla_norm, w_a, w_b, bias, r_col, gate_col, tm):
    n = pm.shape[0]
    d = w_a.shape[1]
    val_w = o.shape[1]
    resident = dict(pipeline_mode=pl.Buffered(1))
    return pl.pallas_call(
        _mix_kernel,
        grid=(n // tm,),
        in_specs=[
            pl.BlockSpec((tm, pm.shape[1]), lambda i: (i, 0)),
            pl.BlockSpec((tm, val_w), lambda i: (i, 0)),
            pl.BlockSpec((tm, val_w), lambda i: (i, r_col // val_w)),
            pl.BlockSpec((1, gla_norm.shape[1]), lambda i: (0, 0)),
            pl.BlockSpec(w_a.shape, lambda i: (0, 0), **resident),
            pl.BlockSpec(w_b.shape, lambda i: (0, 0), **resident),
            pl.BlockSpec((tm, d), lambda i: (i, gate_col // d)),
            pl.BlockSpec((tm, d), lambda i: (i, gate_col // d + 1)),
            pl.BlockSpec((2, d), lambda i: (0, 0)),
        ],
        out_specs=pl.BlockSpec((tm, d), lambda i: (i, 0)),
        out_shape=jax.ShapeDtypeStruct((n, d), BF16),
        compiler_params=_params(("parallel",)),
        name="branch_mix",
    )(pm, o, z, gla_norm, w_a, w_b, z, z, bias)


def _outproj_kernel(m_ref, w_ref, x_ref, g_ref, o_ref):
    y = _dot(m_ref[...], w_ref[...])
    o_ref[...] = x_ref[...] + _rms(y, g_ref[...])


def _outproj(mixed, w_out, x2, g, tm):
    n, d = x2.shape
    return pl.pallas_call(
        _outproj_kernel,
        grid=(n // tm,),
        in_specs=[
            pl.BlockSpec((tm, d), lambda i: (i, 0)),
            pl.BlockSpec((d, d), lambda i: (0, 0)),
            pl.BlockSpec((tm, d), lambda i: (i, 0)),
            pl.BlockSpec((1, d), lambda i: (0, 0)),
        ],
        out_specs=pl.BlockSpec((tm, d), lambda i: (i, 0)),
        out_shape=jax.ShapeDtypeStruct((n, d), F32),
        compiler_params=_params(("parallel",)),
        name="out_proj",
    )(mixed, w_out, x2, g)


def _ffn_kernel(x_ref, gpre_ref, wg_ref, wu_ref, wd_ref, gpost_ref, o_ref, h_ref, acc_ref):
    j = pl.program_id(1)

    @pl.when(j == 0)
    def _():
        h_ref[...] = _rms(x_ref[...], gpre_ref[...]).astype(BF16)
        acc_ref[...] = jnp.zeros_like(acc_ref)

    h = h_ref[...]
    gate = _dot(h, wg_ref[...])
    up = _dot(h, wu_ref[...])
    act = (gate * jax.nn.sigmoid(gate) * up).astype(BF16)
    acc_ref[...] += _dot(act, wd_ref[...])

    @pl.when(j == pl.num_programs(1) - 1)
    def _():
        o_ref[...] = x_ref[...] + _rms(acc_ref[...], gpost_ref[...])


def _ffn(x1, g_pre, w_gate, w_up, w_down, g_post, tm, tf):
    n, d = x1.shape
    dff = w_gate.shape[1]
    return pl.pallas_call(
        _ffn_kernel,
        grid=(n // tm, dff // tf),
        in_specs=[
            pl.BlockSpec((tm, d), lambda i, j: (i, 0)),
            pl.BlockSpec((1, d), lambda i, j: (0, 0)),
            pl.BlockSpec((d, tf), lambda i, j: (0, j)),
            pl.BlockSpec((d, tf), lambda i, j: (0, j)),
            pl.BlockSpec((tf, d), lambda i, j: (j, 0)),
            pl.BlockSpec((1, d), lambda i, j: (0, 0)),
        ],
        out_specs=pl.BlockSpec((tm, d), lambda i, j: (i, 0)),
        out_shape=jax.ShapeDtypeStruct((n, d), F32),
        scratch_shapes=[pltpu.VMEM((tm, d), BF16), pltpu.VMEM((tm, d), F32)],
        compiler_params=_params(("parallel", "arbitrary")),
        name="ffn",
    )(x1, g_pre, w_gate, w_up, w_down, g_post)


def _layer(x2, batch, seq, norm_mix_pre, w_in, w_gate_up, b_gate, w_pool, pool_scale, gla_norm,
           w_branch_a, w_branch_b, b_branch_gates, w_out, norm_mix_post,
           norm_ffn_pre, w_ffn_gate, w_ffn_up, w_ffn_down, norm_ffn_post):
    d = x2.shape[1]
    pool_w = w_branch_a.shape[0]
    key_w = w_gate_up.shape[1]
    val_w = w_branch_b.shape[0]
    rank = w_gate_up.shape[0]

    o_p, o_q, o_k = 0, pool_w, pool_w + key_w
    o_v = o_k + key_w
    o_g = o_v + val_w
    o_r = o_g + rank
    o_gate = o_r + val_w
    w_in_b = w_in.astype(BF16)
    w_main = jnp.concatenate([w_in_b[:, o_v:o_g], w_in_b[:, o_r:], w_in_b[:, o_p:o_v]], axis=1)
    c_v, c_r, c_gate = 0, val_w, 2 * val_w
    c_p = c_gate + 2 * d
    c_q, c_k = c_p + pool_w, c_p + pool_w + key_w
    w_glr = jnp.pad(w_in_b[:, o_g:o_r], ((0, 0), (0, LANES - rank)))
    w_gu = jnp.pad(w_gate_up, ((0, LANES - rank), (0, 0))).astype(BF16)

    z, glr = _inproj(x2, norm_mix_pre[None, :], w_main, w_glr, tm=1024, tn=1024)
    pm = _pool(z, w_pool.astype(BF16), pool_scale[None, :], batch, seq, c_p // pool_w, tt=512)
    o = _gla(z, glr, w_gu, b_gate[None, :], val_w, batch, seq, (c_q, c_k, c_v), tt=256)
    mixed = _mix(pm, o, z, gla_norm[None, :], w_branch_a.astype(BF16), w_branch_b.astype(BF16),
                 b_branch_gates, c_r, c_gate, tm=512)
    x1 = _outproj(mixed, w_out.astype(BF16), x2, norm_mix_post[None, :], tm=512)
    return _ffn(x1, norm_ffn_pre[None, :], w_ffn_gate.astype(BF16), w_ffn_up.astype(BF16),
                w_ffn_down.astype(BF16), norm_ffn_post[None, :], tm=512, tf=512)


def kernel(x, norm_mix_pre, w_in, w_gate_up, b_gate, w_pool, pool_scale, gla_norm, w_branch_a,
           w_branch_b, b_branch_gates, w_out, norm_mix_post, norm_ffn_pre, w_ffn_gate, w_ffn_up,
           w_ffn_down, norm_ffn_post):
    batch, seq, d = x.shape
    x2 = x.reshape(batch * seq, d)
    params = (norm_mix_pre, w_in, w_gate_up, b_gate, w_pool, pool_scale, gla_norm, w_branch_a,
              w_branch_b, b_branch_gates, w_out, norm_mix_post, norm_ffn_pre, w_ffn_gate,
              w_ffn_up, w_ffn_down, norm_ffn_post)
    for layer in range(norm_mix_pre.shape[0]):
        x2 = _layer(x2, batch, seq, *(p[layer] for p in params))
    return x2.reshape(batch, seq, d)
```

```python
import functools

import jax
import jax.numpy as jnp
from jax import lax
from jax.experimental import pallas as pl
from jax.experimental.pallas import tpu as pltpu

F32 = jnp.float32
BF16 = jnp.bfloat16

EPS = 1e-6
POOL_WINDOWS = (2, 4, 8, 16)
POOL_HALO = 16
GLA_HEADS = 4
GLA_GATE_TAU = 16.0
GLA_CHUNK = 128
LOG2E = 1.4426950408889634
SUBLANES = 8
LANES = 128
VMEM_LIMIT = 56 * 1024 * 1024


def _params(sem):
    return pltpu.CompilerParams(dimension_semantics=sem, vmem_limit_bytes=VMEM_LIMIT)


def _rms(x, g):
    ms = jnp.mean(x * x, axis=-1, keepdims=True)
    return x * lax.rsqrt(ms + EPS) * g


def _dot(a, b):
    return jnp.dot(a, b, preferred_element_type=F32)


def _inproj_kernel(x_ref, g_ref, wlo_ref, whi_ref, wglr_ref, z_ref, glr_ref, h_ref, *, n_lo, splits):
    j = pl.program_id(1)

    @pl.when(j == 0)
    def _():
        rb = x_ref.shape[0] // splits
        for r0 in range(0, x_ref.shape[0], rb):
            rows = slice(r0, r0 + rb)
            hb = _rms(x_ref[rows, :], g_ref[...]).astype(BF16)
            h_ref[rows, :] = hb
            glr_ref[rows, :] = _dot(hb, wglr_ref[...])
            z_ref[rows, :] = _dot(hb, wlo_ref[...]).astype(BF16)

    @pl.when((j > 0) & (j < n_lo))
    def _():
        z_ref[...] = _dot(h_ref[...], wlo_ref[...]).astype(BF16)

    @pl.when(j >= n_lo)
    def _():
        z_ref[...] = _dot(h_ref[...], whi_ref[...]).astype(BF16)


def _inproj(x2, g, w_lo, w_hi, w_glr, n_lo, out_tile, tm, tn):
    n, d = x2.shape
    n_hi = w_hi.shape[1] // tn
    return pl.pallas_call(
        functools.partial(_inproj_kernel, n_lo=n_lo, splits=4),
        grid=(n // tm, n_lo + n_hi),
        in_specs=[
            pl.BlockSpec((tm, d), lambda i, j: (i, 0)),
            pl.BlockSpec((1, d), lambda i, j: (0, 0)),
            pl.BlockSpec((d, tn), lambda i, j: (0, jnp.minimum(j, n_lo - 1))),
            pl.BlockSpec((d, tn), lambda i, j: (0, jnp.maximum(j - n_lo, 0))),
            pl.BlockSpec((d, LANES), lambda i, j: (0, 0)),
        ],
        out_specs=[
            pl.BlockSpec((tm, tn), lambda i, j: (i, out_tile(j))),
            pl.BlockSpec((tm, LANES), lambda i, j: (i, 0)),
        ],
        out_shape=[
            jax.ShapeDtypeStruct((n, (n_lo + n_hi) * tn), BF16),
            jax.ShapeDtypeStruct((n, LANES), F32),
        ],
        scratch_shapes=[pltpu.VMEM((tm, d), BF16)],
        compiler_params=_params(("parallel", "arbitrary")),
        name="in_proj",
    )(x2, g, w_lo, w_hi, w_glr)


def _pool_kernel(p_ref, wp_ref, sc_ref, o_ref, carry_ref, *, tt):
    t = pl.program_id(1)

    @pl.when(t == 0)
    def _():
        carry_ref[...] = jnp.zeros_like(carry_ref)

    p = p_ref[...].astype(F32)
    ext = jnp.concatenate([carry_ref[...], p], axis=0)
    carry_ref[...] = p[tt - POOL_HALO:, :]
    pos = t * tt + lax.broadcasted_iota(jnp.int32, (tt, 1), 0)
    gd = wp_ref.shape[1]
    outs = []
    for g, w in enumerate(POOL_WINDOWS):
        s = ext[:, g * gd:(g + 1) * gd]
        shift = 1
        while shift < w:
            s = s + pltpu.roll(s, shift, 0)
            shift *= 2
        count = jnp.minimum(pos + 1, w).astype(F32)
        d = s[POOL_HALO:, :] / count - p[:, g * gd:(g + 1) * gd]
        y = _dot(d.astype(BF16), wp_ref[g])
        outs.append(y * sc_ref[:, g * gd:(g + 1) * gd])
    o_ref[...] = jnp.concatenate(outs, axis=1).astype(BF16)


def _pool(z, w_pool, scale, batch, seq, col_block, tt):
    width = w_pool.shape[0] * w_pool.shape[1]
    nt = seq // tt
    return pl.pallas_call(
        functools.partial(_pool_kernel, tt=tt),
        grid=(batch, nt),
        in_specs=[
            pl.BlockSpec((tt, width), lambda b, t: (b * nt + t, col_block)),
            pl.BlockSpec(w_pool.shape, lambda b, t: (0, 0, 0)),
            pl.BlockSpec((1, width), lambda b, t: (0, 0)),
        ],
        out_specs=pl.BlockSpec((tt, width), lambda b, t: (b * nt + t, 0)),
        out_shape=jax.ShapeDtypeStruct((batch * seq, width), BF16),
        scratch_shapes=[pltpu.VMEM((POOL_HALO, width), F32)],
        compiler_params=_params(("parallel", "arbitrary")),
        name="pool_mixer",
    )(z, w_pool, scale)


_NT = (((1,), (1,)), ((), ()))
_TN = (((0,), (0,)), ((), ()))


def _store_tiles(ref, x):
    for j in range(ref.shape[0]):
        ref[j] = x[:, j * LANES:(j + 1) * LANES]


def _rows(ref, r0, n):
    return jnp.concatenate([ref[j, pl.ds(r0, n), :] for j in range(ref.shape[0])], axis=1)


def _bcast_row(ref, r, n):
    return jnp.concatenate([ref[j, pl.ds(r, n, stride=0), :] for j in range(ref.shape[0])], axis=1)


def _gla_kernel(q_ref, k_ref, v_ref, glr_ref, wgu_ref, bg_ref,
                o_ref, s_ref, g_scr, q_scr, k_scr, *, tt, scale):
    c = GLA_CHUNK
    kw = q_ref.shape[1]
    dk = kw // GLA_HEADS
    dv = v_ref.shape[1] // GLA_HEADS
    heads = range(GLA_HEADS)

    @pl.when(pl.program_id(1) == 0)
    def _():
        s_ref[...] = jnp.zeros_like(s_ref)

    row = lax.broadcasted_iota(jnp.int32, (c, c), 0)
    col = lax.broadcasted_iota(jnp.int32, (c, c), 1)
    tri = (row >= col).astype(BF16)
    sel_r = lax.broadcasted_iota(jnp.int32, (LANES, c), 0)
    sel_c = lax.broadcasted_iota(jnp.int32, (LANES, c), 1)
    sel = (sel_r == (sel_c % SUBLANES)).astype(BF16)
    lane = lax.broadcasted_iota(jnp.int32, (SUBLANES, LANES), 1)
    sub = lax.broadcasted_iota(jnp.int32, (SUBLANES, LANES), 0)
    slot_mask = [(lane == s) & (sub >= s) for s in range(SUBLANES)]
    in_group = ((row // SUBLANES) == (col // SUBLANES)) & (row >= col)
    level_sizes = []
    b = c // 2
    while b >= 2 * SUBLANES:
        level_sizes.append(b)
        b //= 2
    same_block = [(row // b) == (col // b) for b in level_sizes]

    for ci in range(tt // c):
        rows = slice(ci * c, (ci + 1) * c)
        u = _dot(glr_ref[rows, :].astype(BF16), wgu_ref[...]) + bg_ref[...]
        la = (jnp.minimum(u, 0.0) - jnp.log(1.0 + jnp.exp(-jnp.abs(u)))) * (LOG2E / GLA_GATE_TAU)
        la_hi = la.astype(BF16)
        la_lo = (la - la_hi.astype(F32)).astype(BF16)
        gcum = _dot(tri, la_hi) + _dot(tri, la_lo)
        _store_tiles(g_scr, gcum)
        _store_tiles(q_scr, q_ref[rows, :].astype(F32) * scale)
        _store_tiles(k_scr, k_ref[rows, :].astype(F32))

        def level(b):
            half = b // 2
            qs, ks = [], []
            for s in range(0, c, b):
                gref = _bcast_row(g_scr, s + half, half)
                ks.append((_rows(k_scr, s, half) * jnp.exp2(gref - _rows(g_scr, s, half))).astype(BF16))
                ks.append(jnp.zeros((half, kw), BF16))
                qs.append(jnp.zeros((half, kw), BF16))
                qs.append((_rows(q_scr, s + half, half)
                           * jnp.exp2(_rows(g_scr, s + half, half) - gref)).astype(BF16))
            qb, kb = jnp.concatenate(qs, 0), jnp.concatenate(ks, 0)
            return [lax.dot_general(qb[:, h * dk:(h + 1) * dk], kb[:, h * dk:(h + 1) * dk], _NT,
                                    preferred_element_type=F32) for h in heads]

        a = level(c)
        for b, mask in zip(level_sizes, same_block):
            a = [jnp.where(mask, new, old) for new, old in zip(level(b), a)]

        groups = [[] for _ in heads]
        for g0 in range(0, c, SUBLANES):
            qg, gg = _rows(q_scr, g0, SUBLANES), _rows(g_scr, g0, SUBLANES)
            slots = [jnp.zeros((SUBLANES, LANES), F32) for _ in heads]
            for s in range(SUBLANES):
                p = qg * _bcast_row(k_scr, g0 + s, SUBLANES) * jnp.exp2(gg - _bcast_row(g_scr, g0 + s, SUBLANES))
                for h in heads:
                    score = jnp.sum(p[:, h * dk:(h + 1) * dk], axis=1, keepdims=True)
                    slots[h] = jnp.where(slot_mask[s], score, slots[h])
            for h in heads:
                groups[h].append(slots[h])
        a = [jnp.where(in_group, _dot(jnp.concatenate(groups[h], 0).astype(BF16), sel), a[h]).astype(BF16)
             for h in heads]

        gcum = _rows(g_scr, 0, c)
        g_last8 = _bcast_row(g_scr, c - 1, SUBLANES)
        g_last = jnp.concatenate([g_last8] * (c // SUBLANES), 0)
        qg = (_rows(q_scr, 0, c) * jnp.exp2(gcum)).astype(BF16)
        kd = (_rows(k_scr, 0, c) * jnp.exp2(g_last - gcum)).astype(BF16)
        decay_t = jnp.concatenate([jnp.exp2(g_last8)] * (LANES // SUBLANES), 0)
        for h in heads:
            kc = slice(h * dk, (h + 1) * dk)
            vc = slice(h * dv, (h + 1) * dv)
            v = v_ref[rows, vc]
            s_old = s_ref[h]
            o = _dot(qg[:, kc], s_old.astype(BF16)) + _dot(a[h], v)
            decay = decay_t[:, kc].T
            decay = jnp.concatenate([decay] * (dv // LANES), axis=1)
            s_ref[h] = decay * s_old + lax.dot_general(kd[:, kc], v, _TN, preferred_element_type=F32)
            o_ref[rows, vc] = o.astype(BF16)


def _gla(z, glr, w_gate_up, b_gate, val_w, batch, seq, cols, tt):
    key_w = w_gate_up.shape[1]
    dk = key_w // GLA_HEADS
    dv = val_w // GLA_HEADS
    nt = seq // tt
    q0, k0, v0 = cols
    tiles = key_w // LANES
    return pl.pallas_call(
        functools.partial(_gla_kernel, tt=tt, scale=dk ** -0.5),
        grid=(batch, nt),
        in_specs=[
            pl.BlockSpec((tt, key_w), lambda b, t: (b * nt + t, q0 // key_w)),
            pl.BlockSpec((tt, key_w), lambda b, t: (b * nt + t, k0 // key_w)),
            pl.BlockSpec((tt, val_w), lambda b, t: (b * nt + t, v0 // val_w)),
            pl.BlockSpec((tt, LANES), lambda b, t: (b * nt + t, 0)),
            pl.BlockSpec((LANES, key_w), lambda b, t: (0, 0)),
            pl.BlockSpec((1, key_w), lambda b, t: (0, 0)),
        ],
        out_specs=pl.BlockSpec((tt, val_w), lambda b, t: (b * nt + t, 0)),
        out_shape=jax.ShapeDtypeStruct((batch * seq, val_w), BF16),
        scratch_shapes=[
            pltpu.VMEM((GLA_HEADS, dk, dv), F32),
            pltpu.VMEM((tiles, GLA_CHUNK, LANES), F32),
            pltpu.VMEM((tiles, GLA_CHUNK, LANES), F32),
            pltpu.VMEM((tiles, GLA_CHUNK, LANES), F32),
        ],
        compiler_params=_params(("parallel", "arbitrary")),
        name="gla",
    )(z, z, z, glr, w_gate_up, b_gate)


def _mix_kernel(pm_ref, o_ref, r_ref, gn_ref, wa_ref, wb_ref, ga_ref, gb_ref, bias_ref,
                out_ref):
    dv = gn_ref.shape[1]
    y_b = None
    for h in range(o_ref.shape[1] // dv):
        vc = slice(h * dv, (h + 1) * dv)
        r = r_ref[:, vc].astype(F32)
        on = (_rms(o_ref[:, vc].astype(F32), gn_ref[...]) * (r * jax.nn.sigmoid(r))).astype(BF16)
        part = _dot(on, wb_ref[vc, :])
        y_b = part if y_b is None else y_b + part
    y_a = _dot(pm_ref[...], wa_ref[...])
    gate_a = jax.nn.sigmoid(ga_ref[...].astype(F32) + bias_ref[0:1, :])
    gate_b = jax.nn.sigmoid(gb_ref[...].astype(F32) + bias_ref[1:2, :])
    out_ref[...] = (gate_a * y_a + gate_b * y_b).astype(BF16)


def _mix(pm, o, z, gla_norm, w_a, w_b, bias, r_col, gate_col, tm):
    n = pm.shape[0]
    d = w_a.shape[1]
    val_w = o.shape[1]
    resident = dict(pipeline_mode=pl.Buffered(1))
    return pl.pallas_call(
        _mix_kernel,
        grid=(n // tm,),
        in_specs=[
            pl.BlockSpec((tm, pm.shape[1]), lambda i: (i, 0)),
            pl.BlockSpec((tm, val_w), lambda i: (i, 0)),
            pl.BlockSpec((tm, val_w), lambda i: (i, r_col // val_w)),
            pl.BlockSpec((1, gla_norm.shape[1]), lambda i: (0, 0)),
            pl.BlockSpec(w_a.shape, lambda i: (0, 0), **resident),
            pl.BlockSpec(w_b.shape, lambda i: (0, 0), **resident),
            pl.BlockSpec((tm, d), lambda i: (i, gate_col // d)),
            pl.BlockSpec((tm, d), lambda i: (i, gate_col // d + 1)),
            pl.BlockSpec((2, d), lambda i: (0, 0)),
        ],
        out_specs=pl.BlockSpec((tm, d), lambda i: (i, 0)),
        out_shape=jax.ShapeDtypeStruct((n, d), BF16),
        compiler_params=_params(("parallel",)),
        name="branch_mix",
    )(pm, o, z, gla_norm, w_a, w_b, z, z, bias)


def _outproj_kernel(m_ref, w_ref, x_ref, g_ref, o_ref):
    y = _dot(m_ref[...], w_ref[...])
    o_ref[...] = x_ref[...] + _rms(y, g_ref[...])


def _outproj(mixed, w_out, x2, g, tm):
    n, d = x2.shape
    return pl.pallas_call(
        _outproj_kernel,
        grid=(n // tm,),
        in_specs=[
            pl.BlockSpec((tm, d), lambda i: (i, 0)),
            pl.BlockSpec((d, d), lambda i: (0, 0)),
            pl.BlockSpec((tm, d), lambda i: (i, 0)),
            pl.BlockSpec((1, d), lambda i: (0, 0)),
        ],
        out_specs=pl.BlockSpec((tm, d), lambda i: (i, 0)),
        out_shape=jax.ShapeDtypeStruct((n, d), F32),
        compiler_params=_params(("parallel",)),
        name="out_proj",
    )(mixed, w_out, x2, g)


def _ffn_kernel(x_ref, gpre_ref, wg_ref, wu_ref, wd_ref, gpost_ref, o_ref, h_ref, acc_ref):
    j = pl.program_id(1)

    @pl.when(j == 0)
    def _():
        h_ref[...] = _rms(x_ref[...], gpre_ref[...]).astype(BF16)
        acc_ref[...] = jnp.zeros_like(acc_ref)

    h = h_ref[...]
    gate = _dot(h, wg_ref[...])
    up = _dot(h, wu_ref[...])
    act = (gate * jax.nn.sigmoid(gate) * up).astype(BF16)
    acc_ref[...] += _dot(act, wd_ref[...])

    @pl.when(j == pl.num_programs(1) - 1)
    def _():
        o_ref[...] = x_ref[...] + _rms(acc_ref[...], gpost_ref[...])


def _ffn(x1, g_pre, w_gate, w_up, w_down, g_post, tm, tf):
    n, d = x1.shape
    dff = w_gate.shape[1]
    return pl.pallas_call(
        _ffn_kernel,
        grid=(n // tm, dff // tf),
        in_specs=[
            pl.BlockSpec((tm, d), lambda i, j: (i, 0)),
            pl.BlockSpec((1, d), lambda i, j: (0, 0)),
            pl.BlockSpec((d, tf), lambda i, j: (0, j)),
            pl.BlockSpec((d, tf), lambda i, j: (0, j)),
            pl.BlockSpec((tf, d), lambda i, j: (j, 0)),
            pl.BlockSpec((1, d), lambda i, j: (0, 0)),
        ],
        out_specs=pl.BlockSpec((tm, d), lambda i, j: (i, 0)),
        out_shape=jax.ShapeDtypeStruct((n, d), F32),
        scratch_shapes=[pltpu.VMEM((tm, d), BF16), pltpu.VMEM((tm, d), F32)],
        compiler_params=_params(("parallel", "arbitrary")),
        name="ffn",
    )(x1, g_pre, w_gate, w_up, w_down, g_post)


def _layer(x2, batch, seq, norm_mix_pre, w_in, w_gate_up, b_gate, w_pool, pool_scale, gla_norm,
           w_branch_a, w_branch_b, b_branch_gates, w_out, norm_mix_post,
           norm_ffn_pre, w_ffn_gate, w_ffn_up, w_ffn_down, norm_ffn_post):
    d = x2.shape[1]
    pool_w = w_branch_a.shape[0]
    key_w = w_gate_up.shape[1]
    val_w = w_branch_b.shape[0]
    rank = w_gate_up.shape[0]

    o_p, o_q, o_k = 0, pool_w, pool_w + key_w
    o_v = o_k + key_w
    o_g = o_v + val_w
    o_r = o_g + rank
    o_gate = o_r + val_w
    tn = 1024
    w_in_b = w_in.astype(BF16)
    w_hi = w_in_b[:, o_r:]
    n_lo = o_g // tn
    n_front = o_v // tn
    c_v, c_r, c_gate = 0, val_w, 2 * val_w
    c_p = c_gate + 2 * d
    c_q, c_k = c_p + pool_w, c_p + pool_w + key_w
    out_tile = lambda j: jnp.where(j < n_front, j + c_p // tn, j - n_front)
    w_glr = jnp.pad(w_in_b[:, o_g:o_r], ((0, 0), (0, LANES - rank)))
    w_gu = jnp.pad(w_gate_up, ((0, LANES - rank), (0, 0))).astype(BF16)

    z, glr = _inproj(x2, norm_mix_pre[None, :], w_in_b, w_hi, w_glr, n_lo, out_tile, tm=1024, tn=tn)
    pm = _pool(z, w_pool.astype(BF16), pool_scale[None, :], batch, seq, c_p // pool_w, tt=2048)
    o = _gla(z, glr, w_gu, b_gate[None, :], val_w, batch, seq, (c_q, c_k, c_v), tt=256)
    mixed = _mix(pm, o, z, gla_norm[None, :], w_branch_a.astype(BF16), w_branch_b.astype(BF16),
                 b_branch_gates, c_r, c_gate, tm=512)
    x1 = _outproj(mixed, w_out.astype(BF16), x2, norm_mix_post[None, :], tm=512)
    return _ffn(x1, norm_ffn_pre[None, :], w_ffn_gate.astype(BF16), w_ffn_up.astype(BF16),
                w_ffn_down.astype(BF16), norm_ffn_post[None, :], tm=512, tf=512)


def kernel(x, norm_mix_pre, w_in, w_gate_up, b_gate, w_pool, pool_scale, gla_norm, w_branch_a,
           w_branch_b, b_branch_gates, w_out, norm_mix_post, norm_ffn_pre, w_ffn_gate, w_ffn_up,
           w_ffn_down, norm_ffn_post):
    batch, seq, d = x.shape
    x2 = x.reshape(batch * seq, d)
    params = (norm_mix_pre, w_in, w_gate_up, b_gate, w_pool, pool_scale, gla_norm, w_branch_a,
              w_branch_b, b_branch_gates, w_out, norm_mix_post, norm_ffn_pre, w_ffn_gate,
              w_ffn_up, w_ffn_down, norm_ffn_post)
    for layer in range(norm_mix_pre.shape[0]):
        x2 = _layer(x2, batch, seq, *(p[layer] for p in params))
    return x2.reshape(batch, seq, d)
```

```python
import functools

import jax
import jax.numpy as jnp
from jax import lax
from jax.experimental import pallas as pl
from jax.experimental.pallas import tpu as pltpu

F32 = jnp.float32
BF16 = jnp.bfloat16

EPS = 1e-6
POOL_WINDOWS = (2, 4, 8, 16)
POOL_HALO = 16
GLA_HEADS = 4
GLA_GATE_TAU = 16.0
GLA_CHUNK = 128
LOG2E = 1.4426950408889634
SUBLANES = 8
LANES = 128
VMEM_LIMIT = 56 * 1024 * 1024


def _params(sem):
    return pltpu.CompilerParams(dimension_semantics=sem, vmem_limit_bytes=VMEM_LIMIT)


def _rms(x, g):
    ms = jnp.mean(x * x, axis=-1, keepdims=True)
    return x * lax.rsqrt(ms + EPS) * g


def _dot(a, b):
    return jnp.dot(a, b, preferred_element_type=F32)


def _inproj_kernel(x_ref, g_ref, wlo_ref, whi_ref, wglr_ref, z_ref, glr_ref, h_ref, *, n_lo, splits):
    j = pl.program_id(1)

    @pl.when(j == 0)
    def _():
        rb = x_ref.shape[0] // splits
        for r0 in range(0, x_ref.shape[0], rb):
            rows = slice(r0, r0 + rb)
            hb = _rms(x_ref[rows, :], g_ref[...]).astype(BF16)
            h_ref[rows, :] = hb
            glr_ref[rows, :] = _dot(hb, wglr_ref[...])
            z_ref[rows, :] = _dot(hb, wlo_ref[...]).astype(BF16)

    @pl.when((j > 0) & (j < n_lo))
    def _():
        z_ref[...] = _dot(h_ref[...], wlo_ref[...]).astype(BF16)

    @pl.when(j >= n_lo)
    def _():
        z_ref[...] = _dot(h_ref[...], whi_ref[...]).astype(BF16)


def _inproj(x2, g, w_lo, w_hi, w_glr, n_lo, out_tile, tm, tn):
    n, d = x2.shape
    n_hi = w_hi.shape[1] // tn

    def lo_tile(i, j):
        t = jnp.minimum(j, n_lo - 1)
        return jnp.where(i % 2 == 0, t, n_lo - 1 - t)

    def hi_tile(i, j):
        t = jnp.maximum(j - n_lo, 0)
        return jnp.where(i % 2 == 0, t, n_hi - 1 - t)

    def z_tile(i, j):
        return out_tile(jnp.where(j < n_lo, lo_tile(i, j), n_lo + hi_tile(i, j)))

    return pl.pallas_call(
        functools.partial(_inproj_kernel, n_lo=n_lo, splits=4),
        grid=(n // tm, n_lo + n_hi),
        in_specs=[
            pl.BlockSpec((tm, d), lambda i, j: (i, 0)),
            pl.BlockSpec((1, d), lambda i, j: (0, 0)),
            pl.BlockSpec((d, tn), lambda i, j: (0, lo_tile(i, j))),
            pl.BlockSpec((d, tn), lambda i, j: (0, hi_tile(i, j))),
            pl.BlockSpec((d, LANES), lambda i, j: (0, 0)),
        ],
        out_specs=[
            pl.BlockSpec((tm, tn), lambda i, j: (i, z_tile(i, j))),
            pl.BlockSpec((tm, LANES), lambda i, j: (i, 0)),
        ],
        out_shape=[
            jax.ShapeDtypeStruct((n, (n_lo + n_hi) * tn), BF16),
            jax.ShapeDtypeStruct((n, LANES), F32),
        ],
        scratch_shapes=[pltpu.VMEM((tm, d), BF16)],
        compiler_params=_params(("parallel", "arbitrary")),
        name="in_proj",
    )(x2, g, w_lo, w_hi, w_glr)


def _pool_kernel(p_ref, wp_ref, sc_ref, o_ref, carry_ref, *, tt):
    t = pl.program_id(1)

    @pl.when(t == 0)
    def _():
        carry_ref[...] = jnp.zeros_like(carry_ref)

    p = p_ref[...].astype(F32)
    ext = jnp.concatenate([carry_ref[...], p], axis=0)
    carry_ref[...] = p[tt - POOL_HALO:, :]
    pos = t * tt + lax.broadcasted_iota(jnp.int32, (tt, 1), 0)
    gd = wp_ref.shape[1]
    outs = []
    for g, w in enumerate(POOL_WINDOWS):
        s = ext[:, g * gd:(g + 1) * gd]
        shift = 1
        while shift < w:
            s = s + pltpu.roll(s, shift, 0)
            shift *= 2
        count = jnp.minimum(pos + 1, w).astype(F32)
        d = s[POOL_HALO:, :] / count - p[:, g * gd:(g + 1) * gd]
        y = _dot(d.astype(BF16), wp_ref[g])
        outs.append(y * sc_ref[:, g * gd:(g + 1) * gd])
    o_ref[...] = jnp.concatenate(outs, axis=1).astype(BF16)


def _pool(z, w_pool, scale, batch, seq, col_block, tt):
    width = w_pool.shape[0] * w_pool.shape[1]
    nt = seq // tt
    return pl.pallas_call(
        functools.partial(_pool_kernel, tt=tt),
        grid=(batch, nt),
        in_specs=[
            pl.BlockSpec((tt, width), lambda b, t: (b * nt + t, col_block)),
            pl.BlockSpec(w_pool.shape, lambda b, t: (0, 0, 0)),
            pl.BlockSpec((1, width), lambda b, t: (0, 0)),
        ],
        out_specs=pl.BlockSpec((tt, width), lambda b, t: (b * nt + t, 0)),
        out_shape=jax.ShapeDtypeStruct((batch * seq, width), BF16),
        scratch_shapes=[pltpu.VMEM((POOL_HALO, width), F32)],
        compiler_params=_params(("parallel", "arbitrary")),
        name="pool_mixer",
    )(z, w_pool, scale)


_NT = (((1,), (1,)), ((), ()))
_TN = (((0,), (0,)), ((), ()))


def _store_tiles(ref, x):
    for j in range(ref.shape[0]):
        ref[j] = x[:, j * LANES:(j + 1) * LANES]


def _rows(ref, r0, n):
    return jnp.concatenate([ref[j, pl.ds(r0, n), :] for j in range(ref.shape[0])], axis=1)


def _bcast_row(ref, r, n):
    return jnp.concatenate([ref[j, pl.ds(r, n, stride=0), :] for j in range(ref.shape[0])], axis=1)


def _gla_kernel(q_ref, k_ref, v_ref, glr_ref, wgu_ref, bg_ref,
                o_ref, s_ref, g_scr, q_scr, k_scr, *, tt, scale):
    c = GLA_CHUNK
    kw = q_ref.shape[1]
    dk = kw // GLA_HEADS
    dv = v_ref.shape[1] // GLA_HEADS
    heads = range(GLA_HEADS)

    @pl.when(pl.program_id(1) == 0)
    def _():
        s_ref[...] = jnp.zeros_like(s_ref)

    row = lax.broadcasted_iota(jnp.int32, (c, c), 0)
    col = lax.broadcasted_iota(jnp.int32, (c, c), 1)
    tri = (row >= col).astype(BF16)
    sel_r = lax.broadcasted_iota(jnp.int32, (LANES, c), 0)
    sel_c = lax.broadcasted_iota(jnp.int32, (LANES, c), 1)
    sel = (sel_r == (sel_c % SUBLANES)).astype(BF16)
    lane = lax.broadcasted_iota(jnp.int32, (SUBLANES, LANES), 1)
    sub = lax.broadcasted_iota(jnp.int32, (SUBLANES, LANES), 0)
    slot_mask = [(lane == s) & (sub >= s) for s in range(SUBLANES)]
    in_group = ((row // SUBLANES) == (col // SUBLANES)) & (row >= col)
    level_sizes = []
    b = c // 2
    while b >= 2 * SUBLANES:
        level_sizes.append(b)
        b //= 2
    same_block = [(row // b) == (col // b) for b in level_sizes]

    for ci in range(tt // c):
        rows = slice(ci * c, (ci + 1) * c)
        u = _dot(glr_ref[rows, :].astype(BF16), wgu_ref[...]) + bg_ref[...]
        la = (jnp.minimum(u, 0.0) - jnp.log(1.0 + jnp.exp(-jnp.abs(u)))) * (LOG2E / GLA_GATE_TAU)
        la_hi = la.astype(BF16)
        la_lo = (la - la_hi.astype(F32)).astype(BF16)
        gcum = _dot(tri, la_hi) + _dot(tri, la_lo)
        _store_tiles(g_scr, gcum)
        _store_tiles(q_scr, q_ref[rows, :].astype(F32) * scale)
        _store_tiles(k_scr, k_ref[rows, :].astype(F32))

        def level(b):
            half = b // 2
            qs, ks = [], []
            for s in range(0, c, b):
                gref = _bcast_row(g_scr, s + half, half)
                ks.append((_rows(k_scr, s, half) * jnp.exp2(gref - _rows(g_scr, s, half))).astype(BF16))
                ks.append(jnp.zeros((half, kw), BF16))
                qs.append(jnp.zeros((half, kw), BF16))
                qs.append((_rows(q_scr, s + half, half)
                           * jnp.exp2(_rows(g_scr, s + half, half) - gref)).astype(BF16))
            qb, kb = jnp.concatenate(qs, 0), jnp.concatenate(ks, 0)
            return [lax.dot_general(qb[:, h * dk:(h + 1) * dk], kb[:, h * dk:(h + 1) * dk], _NT,
                                    preferred_element_type=F32) for h in heads]

        a = level(c)
        for b, mask in zip(level_sizes, same_block):
            a = [jnp.where(mask, new, old) for new, old in zip(level(b), a)]

        groups = [[] for _ in heads]
        for g0 in range(0, c, SUBLANES):
            qg, gg = _rows(q_scr, g0, SUBLANES), _rows(g_scr, g0, SUBLANES)
            slots = [jnp.zeros((SUBLANES, LANES), F32) for _ in heads]
            for s in range(SUBLANES):
                p = qg * _bcast_row(k_scr, g0 + s, SUBLANES) * jnp.exp2(gg - _bcast_row(g_scr, g0 + s, SUBLANES))
                for h in heads:
                    score = jnp.sum(p[:, h * dk:(h + 1) * dk], axis=1, keepdims=True)
                    slots[h] = jnp.where(slot_mask[s], score, slots[h])
            for h in heads:
                groups[h].append(slots[h])
        a = [jnp.where(in_group, _dot(jnp.concatenate(groups[h], 0).astype(BF16), sel), a[h]).astype(BF16)
             for h in heads]

        gcum = _rows(g_scr, 0, c)
        g_last8 = _bcast_row(g_scr, c - 1, SUBLANES)
        g_last = jnp.concatenate([g_last8] * (c // SUBLANES), 0)
        qg = (_rows(q_scr, 0, c) * jnp.exp2(gcum)).astype(BF16)
        kd = (_rows(k_scr, 0, c) * jnp.exp2(g_last - gcum)).astype(BF16)
        decay_t = jnp.concatenate([jnp.exp2(g_last8)] * (LANES // SUBLANES), 0)
        for h in heads:
            kc = slice(h * dk, (h + 1) * dk)
            vc = slice(h * dv, (h + 1) * dv)
            v = v_ref[rows, vc]
            s_old = s_ref[h]
            o = _dot(qg[:, kc], s_old.astype(BF16)) + _dot(a[h], v)
            decay = decay_t[:, kc].T
            decay = jnp.concatenate([decay] * (dv // LANES), axis=1)
            s_ref[h] = decay * s_old + lax.dot_general(kd[:, kc], v, _TN, preferred_element_type=F32)
            o_ref[rows, vc] = o.astype(BF16)


def _gla(z, glr, w_gate_up, b_gate, val_w, batch, seq, cols, tt):
    key_w = w_gate_up.shape[1]
    dk = key_w // GLA_HEADS
    dv = val_w // GLA_HEADS
    nt = seq // tt
    q0, k0, v0 = cols
    tiles = key_w // LANES
    return pl.pallas_call(
        functools.partial(_gla_kernel, tt=tt, scale=dk ** -0.5),
        grid=(batch, nt),
        in_specs=[
            pl.BlockSpec((tt, key_w), lambda b, t: (b * nt + t, q0 // key_w)),
            pl.BlockSpec((tt, key_w), lambda b, t: (b * nt + t, k0 // key_w)),
            pl.BlockSpec((tt, val_w), lambda b, t: (b * nt + t, v0 // val_w)),
            pl.BlockSpec((tt, LANES), lambda b, t: (b * nt + t, 0)),
            pl.BlockSpec((LANES, key_w), lambda b, t: (0, 0)),
            pl.BlockSpec((1, key_w), lambda b, t: (0, 0)),
        ],
        out_specs=pl.BlockSpec((tt, val_w), lambda b, t: (b * nt + t, 0)),
        out_shape=jax.ShapeDtypeStruct((batch * seq, val_w), BF16),
        scratch_shapes=[
            pltpu.VMEM((GLA_HEADS, dk, dv), F32),
            pltpu.VMEM((tiles, GLA_CHUNK, LANES), F32),
            pltpu.VMEM((tiles, GLA_CHUNK, LANES), F32),
            pltpu.VMEM((tiles, GLA_CHUNK, LANES), F32),
        ],
        compiler_params=_params(("parallel", "arbitrary")),
        name="gla",
    )(z, z, z, glr, w_gate_up, b_gate)


def _mix_kernel(pm_ref, o_ref, r_ref, gn_ref, wa_ref, wb_ref, ga_ref, gb_ref, bias_ref,
                out_ref):
    dv = gn_ref.shape[1]
    y_b = None
    for h in range(o_ref.shape[1] // dv):
        vc = slice(h * dv, (h + 1) * dv)
        r = r_ref[:, vc].astype(F32)
        on = (_rms(o_ref[:, vc].astype(F32), gn_ref[...]) * (r * jax.nn.sigmoid(r))).astype(BF16)
        part = _dot(on, wb_ref[vc, :])
        y_b = part if y_b is None else y_b + part
    y_a = _dot(pm_ref[...], wa_ref[...])
    gate_a = jax.nn.sigmoid(ga_ref[...].astype(F32) + bias_ref[0:1, :])
    gate_b = jax.nn.sigmoid(gb_ref[...].astype(F32) + bias_ref[1:2, :])
    out_ref[...] = (gate_a * y_a + gate_b * y_b).astype(BF16)


def _mix(pm, o, z, gla_norm, w_a, w_b, bias, r_col, gate_col, tm):
    n = pm.shape[0]
    d = w_a.shape[1]
    val_w = o.shape[1]
    resident = dict(pipeline_mode=pl.Buffered(1))
    return pl.pallas_call(
        _mix_kernel,
        grid=(n // tm,),
        in_specs=[
            pl.BlockSpec((tm, pm.shape[1]), lambda i: (i, 0)),
            pl.BlockSpec((tm, val_w), lambda i: (i, 0)),
            pl.BlockSpec((tm, val_w), lambda i: (i, r_col // val_w)),
            pl.BlockSpec((1, gla_norm.shape[1]), lambda i: (0, 0)),
            pl.BlockSpec(w_a.shape, lambda i: (0, 0), **resident),
            pl.BlockSpec(w_b.shape, lambda i: (0, 0), **resident),
            pl.BlockSpec((tm, d), lambda i: (i, gate_col // d)),
            pl.BlockSpec((tm, d), lambda i: (i, gate_col // d + 1)),
            pl.BlockSpec((2, d), lambda i: (0, 0)),
        ],
        out_specs=pl.BlockSpec((tm, d), lambda i: (i, 0)),
        out_shape=jax.ShapeDtypeStruct((n, d), BF16),
        compiler_params=_params(("parallel",)),
        name="branch_mix",
    )(pm, o, z, gla_norm, w_a, w_b, z, z, bias)


def _outproj_kernel(m_ref, w_ref, x_ref, g_ref, o_ref):
    y = _dot(m_ref[...], w_ref[...])
    o_ref[...] = x_ref[...] + _rms(y, g_ref[...])


def _outproj(mixed, w_out, x2, g, tm):
    n, d = x2.shape
    return pl.pallas_call(
        _outproj_kernel,
        grid=(n // tm,),
        in_specs=[
            pl.BlockSpec((tm, d), lambda i: (i, 0)),
            pl.BlockSpec((d, d), lambda i: (0, 0)),
            pl.BlockSpec((tm, d), lambda i: (i, 0)),
            pl.BlockSpec((1, d), lambda i: (0, 0)),
        ],
        out_specs=pl.BlockSpec((tm, d), lambda i: (i, 0)),
        out_shape=jax.ShapeDtypeStruct((n, d), F32),
        compiler_params=_params(("parallel",)),
        name="out_proj",
    )(mixed, w_out, x2, g)


def _ffn_kernel(x_ref, gpre_ref, wg_ref, wu_ref, wd_ref, gpost_ref, o_ref, h_ref, acc_ref):
    j = pl.program_id(1)

    @pl.when(j == 0)
    def _():
        h_ref[...] = _rms(x_ref[...], gpre_ref[...]).astype(BF16)
        acc_ref[...] = jnp.zeros_like(acc_ref)

    h = h_ref[...]
    gate = _dot(h, wg_ref[...])
    up = _dot(h, wu_ref[...])
    act = (gate * jax.nn.sigmoid(gate) * up).astype(BF16)
    acc_ref[...] += _dot(act, wd_ref[...])

    @pl.when(j == pl.num_programs(1) - 1)
    def _():
        o_ref[...] = x_ref[...] + _rms(acc_ref[...], gpost_ref[...])


def _ffn(x1, g_pre, w_gate, w_up, w_down, g_post, tm, tf):
    n, d = x1.shape
    dff = w_gate.shape[1]
    nf = dff // tf
    ft = lambda i, j: jnp.where(i % 2 == 0, j, nf - 1 - j)
    return pl.pallas_call(
        _ffn_kernel,
        grid=(n // tm, nf),
        in_specs=[
            pl.BlockSpec((tm, d), lambda i, j: (i, 0)),
            pl.BlockSpec((1, d), lambda i, j: (0, 0)),
            pl.BlockSpec((d, tf), lambda i, j: (0, ft(i, j))),
            pl.BlockSpec((d, tf), lambda i, j: (0, ft(i, j))),
            pl.BlockSpec((tf, d), lambda i, j: (ft(i, j), 0)),
            pl.BlockSpec((1, d), lambda i, j: (0, 0)),
        ],
        out_specs=pl.BlockSpec((tm, d), lambda i, j: (i, 0)),
        out_shape=jax.ShapeDtypeStruct((n, d), F32),
        scratch_shapes=[pltpu.VMEM((tm, d), BF16), pltpu.VMEM((tm, d), F32)],
        compiler_params=_params(("parallel", "arbitrary")),
        name="ffn",
    )(x1, g_pre, w_gate, w_up, w_down, g_post)


def _layer(x2, batch, seq, norm_mix_pre, w_in, w_gate_up, b_gate, w_pool, pool_scale, gla_norm,
           w_branch_a, w_branch_b, b_branch_gates, w_out, norm_mix_post,
           norm_ffn_pre, w_ffn_gate, w_ffn_up, w_ffn_down, norm_ffn_post):
    d = x2.shape[1]
    pool_w = w_branch_a.shape[0]
    key_w = w_gate_up.shape[1]
    val_w = w_branch_b.shape[0]
    rank = w_gate_up.shape[0]

    o_p, o_q, o_k = 0, pool_w, pool_w + key_w
    o_v = o_k + key_w
    o_g = o_v + val_w
    o_r = o_g + rank
    o_gate = o_r + val_w
    tn = 1024
    w_in_b = w_in.astype(BF16)
    w_hi = w_in_b[:, o_r:]
    n_lo = o_g // tn
    n_front = o_v // tn
    c_v, c_r, c_gate = 0, val_w, 2 * val_w
    c_p = c_gate + 2 * d
    c_q, c_k = c_p + pool_w, c_p + pool_w + key_w
    out_tile = lambda j: jnp.where(j < n_front, j + c_p // tn, j - n_front)
    w_glr = jnp.pad(w_in_b[:, o_g:o_r], ((0, 0), (0, LANES - rank)))
    w_gu = jnp.pad(w_gate_up, ((0, LANES - rank), (0, 0))).astype(BF16)

    z, glr = _inproj(x2, norm_mix_pre[None, :], w_in_b, w_hi, w_glr, n_lo, out_tile, tm=1024, tn=tn)
    pm = _pool(z, w_pool.astype(BF16), pool_scale[None, :], batch, seq, c_p // pool_w, tt=2048)
    o = _gla(z, glr, w_gu, b_gate[None, :], val_w, batch, seq, (c_q, c_k, c_v), tt=256)
    mixed = _mix(pm, o, z, gla_norm[None, :], w_branch_a.astype(BF16), w_branch_b.astype(BF16),
                 b_branch_gates, c_r, c_gate, tm=512)
    x1 = _outproj(mixed, w_out.astype(BF16), x2, norm_mix_post[None, :], tm=512)
    return _ffn(x1, norm_ffn_pre[None, :], w_ffn_gate.astype(BF16), w_ffn_up.astype(BF16),
                w_ffn_down.astype(BF16), norm_ffn_post[None, :], tm=512, tf=512)


def kernel(x, norm_mix_pre, w_in, w_gate_up, b_gate, w_pool, pool_scale, gla_norm, w_branch_a,
           w_branch_b, b_branch_gates, w_out, norm_mix_post, norm_ffn_pre, w_ffn_gate, w_ffn_up,
           w_ffn_down, norm_ffn_post):
    batch, seq, d = x.shape
    x2 = x.reshape(batch * seq, d)
    params = (norm_mix_pre, w_in, w_gate_up, b_gate, w_pool, pool_scale, gla_norm, w_branch_a,
              w_branch_b, b_branch_gates, w_out, norm_mix_post, norm_ffn_pre, w_ffn_gate,
              w_ffn_up, w_ffn_down, norm_ffn_post)
    for layer in range(norm_mix_pre.shape[0]):
        x2 = _layer(x2, batch, seq, *(p[layer] for p in params))
    return x2.reshape(batch, seq, d)
```

```python
import functools

import jax
import jax.numpy as jnp
from jax import lax
from jax.experimental import pallas as pl
from jax.experimental.pallas import tpu as pltpu

F32 = jnp.float32
BF16 = jnp.bfloat16

EPS = 1e-6
POOL_WINDOWS = (2, 4, 8, 16)
POOL_HALO = 16
GLA_HEADS = 4
GLA_GATE_TAU = 16.0
GLA_CHUNK = 128
LOG2E = 1.4426950408889634
SUBLANES = 8
LANES = 128
VMEM_LIMIT = 56 * 1024 * 1024


def _params(sem):
    return pltpu.CompilerParams(dimension_semantics=sem, vmem_limit_bytes=VMEM_LIMIT)


def _rms(x, g):
    ms = jnp.mean(x * x, axis=-1, keepdims=True)
    return x * lax.rsqrt(ms + EPS) * g


def _dot(a, b):
    return jnp.dot(a, b, preferred_element_type=F32)


def _inproj_kernel(x_ref, g_ref, wlo_ref, whi_ref, wglr_ref, z_ref, glr_ref, h_ref, *, n_lo, splits):
    j = pl.program_id(1)

    @pl.when(j == 0)
    def _():
        rb = x_ref.shape[0] // splits
        for r0 in range(0, x_ref.shape[0], rb):
            rows = slice(r0, r0 + rb)
            hb = _rms(x_ref[rows, :], g_ref[...]).astype(BF16)
            h_ref[rows, :] = hb
            glr_ref[rows, :] = _dot(hb, wglr_ref[...])
            z_ref[rows, :] = _dot(hb, wlo_ref[...]).astype(BF16)

    @pl.when((j > 0) & (j < n_lo))
    def _():
        z_ref[...] = _dot(h_ref[...], wlo_ref[...]).astype(BF16)

    @pl.when(j >= n_lo)
    def _():
        z_ref[...] = _dot(h_ref[...], whi_ref[...]).astype(BF16)


def _inproj(x2, g, w_lo, w_hi, w_glr, n_lo, out_tile, tm, tn):
    n, d = x2.shape
    n_hi = w_hi.shape[1] // tn

    def lo_tile(i, j):
        t = jnp.minimum(j, n_lo - 1)
        return jnp.where(i % 2 == 0, t, n_lo - 1 - t)

    def hi_tile(i, j):
        t = jnp.maximum(j - n_lo, 0)
        return jnp.where(i % 2 == 0, t, n_hi - 1 - t)

    def z_tile(i, j):
        return out_tile(jnp.where(j < n_lo, lo_tile(i, j), n_lo + hi_tile(i, j)))

    return pl.pallas_call(
        functools.partial(_inproj_kernel, n_lo=n_lo, splits=4),
        grid=(n // tm, n_lo + n_hi),
        in_specs=[
            pl.BlockSpec((tm, d), lambda i, j: (i, 0)),
            pl.BlockSpec((1, d), lambda i, j: (0, 0)),
            pl.BlockSpec((d, tn), lambda i, j: (0, lo_tile(i, j))),
            pl.BlockSpec((d, tn), lambda i, j: (0, hi_tile(i, j))),
            pl.BlockSpec((d, LANES), lambda i, j: (0, 0)),
        ],
        out_specs=[
            pl.BlockSpec((tm, tn), lambda i, j: (i, z_tile(i, j))),
            pl.BlockSpec((tm, LANES), lambda i, j: (i, 0)),
        ],
        out_shape=[
            jax.ShapeDtypeStruct((n, (n_lo + n_hi) * tn), BF16),
            jax.ShapeDtypeStruct((n, LANES), F32),
        ],
        scratch_shapes=[pltpu.VMEM((tm, d), BF16)],
        compiler_params=_params(("parallel", "arbitrary")),
        name="in_proj",
    )(x2, g, w_lo, w_hi, w_glr)


def _pool_kernel(p_ref, wp_ref, sc_ref, o_ref, carry_ref, *, tt):
    t = pl.program_id(1)

    @pl.when(t == 0)
    def _():
        carry_ref[...] = jnp.zeros_like(carry_ref)

    p = p_ref[...].astype(F32)
    ext = jnp.concatenate([carry_ref[...], p], axis=0)
    carry_ref[...] = p[tt - POOL_HALO:, :]
    pos = t * tt + lax.broadcasted_iota(jnp.int32, (tt, 1), 0)
    gd = wp_ref.shape[1]
    outs = []
    for g, w in enumerate(POOL_WINDOWS):
        s = ext[:, g * gd:(g + 1) * gd]
        shift = 1
        while shift < w:
            s = s + pltpu.roll(s, shift, 0)
            shift *= 2
        count = jnp.minimum(pos + 1, w).astype(F32)
        d = s[POOL_HALO:, :] / count - p[:, g * gd:(g + 1) * gd]
        y = _dot(d.astype(BF16), wp_ref[g])
        outs.append(y * sc_ref[:, g * gd:(g + 1) * gd])
    o_ref[...] = jnp.concatenate(outs, axis=1).astype(BF16)


def _pool(z, w_pool, scale, batch, seq, col_block, tt):
    width = w_pool.shape[0] * w_pool.shape[1]
    nt = seq // tt
    return pl.pallas_call(
        functools.partial(_pool_kernel, tt=tt),
        grid=(batch, nt),
        in_specs=[
            pl.BlockSpec((tt, width), lambda b, t: (b * nt + t, col_block)),
            pl.BlockSpec(w_pool.shape, lambda b, t: (0, 0, 0)),
            pl.BlockSpec((1, width), lambda b, t: (0, 0)),
        ],
        out_specs=pl.BlockSpec((tt, width), lambda b, t: (b * nt + t, 0)),
        out_shape=jax.ShapeDtypeStruct((batch * seq, width), BF16),
        scratch_shapes=[pltpu.VMEM((POOL_HALO, width), F32)],
        compiler_params=_params(("parallel", "arbitrary")),
        name="pool_mixer",
    )(z, w_pool, scale)


_NT = (((1,), (1,)), ((), ()))
_TN = (((0,), (0,)), ((), ()))


def _store_tiles(ref, x):
    for j in range(ref.shape[0]):
        ref[j] = x[:, j * LANES:(j + 1) * LANES]


def _rows(ref, r0, n):
    return jnp.concatenate([ref[j, pl.ds(r0, n), :] for j in range(ref.shape[0])], axis=1)


def _bcast_row(ref, r, n):
    return jnp.concatenate([ref[j, pl.ds(r, n, stride=0), :] for j in range(ref.shape[0])], axis=1)


def _gla_kernel(q_ref, k_ref, v_ref, glr_ref, wgu_ref, bg_ref,
                o_ref, s_ref, g_scr, q_scr, k_scr, *, tt, scale):
    c = GLA_CHUNK
    kw = q_ref.shape[1]
    dk = kw // GLA_HEADS
    dv = v_ref.shape[1] // GLA_HEADS
    heads = range(GLA_HEADS)

    @pl.when(pl.program_id(1) == 0)
    def _():
        s_ref[...] = jnp.zeros_like(s_ref)

    row = lax.broadcasted_iota(jnp.int32, (c, c), 0)
    col = lax.broadcasted_iota(jnp.int32, (c, c), 1)
    tri = (row >= col).astype(BF16)
    sel_r = lax.broadcasted_iota(jnp.int32, (LANES, c), 0)
    sel_c = lax.broadcasted_iota(jnp.int32, (LANES, c), 1)
    sel = (sel_r == (sel_c % SUBLANES)).astype(BF16)
    lane = lax.broadcasted_iota(jnp.int32, (SUBLANES, LANES), 1)
    sub = lax.broadcasted_iota(jnp.int32, (SUBLANES, LANES), 0)
    slot_mask = [(lane == s) & (sub >= s) for s in range(SUBLANES)]
    in_group = ((row // SUBLANES) == (col // SUBLANES)) & (row >= col)
    level_sizes = []
    b = c // 2
    while b >= 2 * SUBLANES:
        level_sizes.append(b)
        b //= 2
    same_block = [(row // b) == (col // b) for b in level_sizes]

    for ci in range(tt // c):
        rows = slice(ci * c, (ci + 1) * c)
        u = _dot(glr_ref[rows, :].astype(BF16), wgu_ref[...]) + bg_ref[...]
        la = (jnp.minimum(u, 0.0) - jnp.log(1.0 + jnp.exp(-jnp.abs(u)))) * (LOG2E / GLA_GATE_TAU)
        la_hi = la.astype(BF16)
        la_lo = (la - la_hi.astype(F32)).astype(BF16)
        gcum = _dot(tri, la_hi) + _dot(tri, la_lo)
        _store_tiles(g_scr, gcum)
        _store_tiles(q_scr, q_ref[rows, :].astype(F32) * scale)
        _store_tiles(k_scr, k_ref[rows, :].astype(F32))

        def level(b):
            half = b // 2
            qs, ks = [], []
            for s in range(0, c, b):
                gref = _bcast_row(g_scr, s + half, half)
                ks.append((_rows(k_scr, s, half) * jnp.exp2(gref - _rows(g_scr, s, half))).astype(BF16))
                ks.append(jnp.zeros((half, kw), BF16))
                qs.append(jnp.zeros((half, kw), BF16))
                qs.append((_rows(q_scr, s + half, half)
                           * jnp.exp2(_rows(g_scr, s + half, half) - gref)).astype(BF16))
            qb, kb = jnp.concatenate(qs, 0), jnp.concatenate(ks, 0)
            return [lax.dot_general(qb[:, h * dk:(h + 1) * dk], kb[:, h * dk:(h + 1) * dk], _NT,
                                    preferred_element_type=F32) for h in heads]

        a = level(c)
        for b, mask in zip(level_sizes, same_block):
            a = [jnp.where(mask, new, old) for new, old in zip(level(b), a)]

        groups = [[] for _ in heads]
        for g0 in range(0, c, SUBLANES):
            qg, gg = _rows(q_scr, g0, SUBLANES), _rows(g_scr, g0, SUBLANES)
            slots = [jnp.zeros((SUBLANES, LANES), F32) for _ in heads]
            for s in range(SUBLANES):
                p = qg * _bcast_row(k_scr, g0 + s, SUBLANES) * jnp.exp2(gg - _bcast_row(g_scr, g0 + s, SUBLANES))
                for h in heads:
                    score = jnp.sum(p[:, h * dk:(h + 1) * dk], axis=1, keepdims=True)
                    slots[h] = jnp.where(slot_mask[s], score, slots[h])
            for h in heads:
                groups[h].append(slots[h])
        a = [jnp.where(in_group, _dot(jnp.concatenate(groups[h], 0).astype(BF16), sel), a[h]).astype(BF16)
             for h in heads]

        gcum = _rows(g_scr, 0, c)
        g_last8 = _bcast_row(g_scr, c - 1, SUBLANES)
        g_last = jnp.concatenate([g_last8] * (c // SUBLANES), 0)
        qg = (_rows(q_scr, 0, c) * jnp.exp2(gcum)).astype(BF16)
        kd = (_rows(k_scr, 0, c) * jnp.exp2(g_last - gcum)).astype(BF16)
        decay_t = jnp.concatenate([jnp.exp2(g_last8)] * (LANES // SUBLANES), 0)
        for h in heads:
            kc = slice(h * dk, (h + 1) * dk)
            vc = slice(h * dv, (h + 1) * dv)
            v = v_ref[rows, vc]
            s_old = s_ref[h]
            o = _dot(qg[:, kc], s_old.astype(BF16)) + _dot(a[h], v)
            decay = decay_t[:, kc].T
            decay = jnp.concatenate([decay] * (dv // LANES), axis=1)
            s_ref[h] = decay * s_old + lax.dot_general(kd[:, kc], v, _TN, preferred_element_type=F32)
            o_ref[rows, vc] = o.astype(BF16)


def _gla(z, glr, w_gate_up, b_gate, val_w, batch, seq, cols, tt):
    key_w = w_gate_up.shape[1]
    dk = key_w // GLA_HEADS
    dv = val_w // GLA_HEADS
    nt = seq // tt
    q0, k0, v0 = cols
    tiles = key_w // LANES
    return pl.pallas_call(
        functools.partial(_gla_kernel, tt=tt, scale=dk ** -0.5),
        grid=(batch, nt),
        in_specs=[
            pl.BlockSpec((tt, key_w), lambda b, t: (b * nt + t, q0 // key_w)),
            pl.BlockSpec((tt, key_w), lambda b, t: (b * nt + t, k0 // key_w)),
            pl.BlockSpec((tt, val_w), lambda b, t: (b * nt + t, v0 // val_w)),
            pl.BlockSpec((tt, LANES), lambda b, t: (b * nt + t, 0)),
            pl.BlockSpec((LANES, key_w), lambda b, t: (0, 0)),
            pl.BlockSpec((1, key_w), lambda b, t: (0, 0)),
        ],
        out_specs=pl.BlockSpec((tt, val_w), lambda b, t: (b * nt + t, 0)),
        out_shape=jax.ShapeDtypeStruct((batch * seq, val_w), BF16),
        scratch_shapes=[
            pltpu.VMEM((GLA_HEADS, dk, dv), F32),
            pltpu.VMEM((tiles, GLA_CHUNK, LANES), F32),
            pltpu.VMEM((tiles, GLA_CHUNK, LANES), F32),
            pltpu.VMEM((tiles, GLA_CHUNK, LANES), F32),
        ],
        compiler_params=_params(("parallel", "arbitrary")),
        name="gla",
    )(z, z, z, glr, w_gate_up, b_gate)


def _mix_kernel(pm_ref, o_ref, r_ref, gn_ref, wa_ref, wb_ref, ga_ref, gb_ref, bias_ref,
                out_ref):
    dv = gn_ref.shape[1]
    y_b = None
    for h in range(o_ref.shape[1] // dv):
        vc = slice(h * dv, (h + 1) * dv)
        r = r_ref[:, vc].astype(F32)
        on = (_rms(o_ref[:, vc].astype(F32), gn_ref[...]) * (r * jax.nn.sigmoid(r))).astype(BF16)
        part = _dot(on, wb_ref[vc, :])
        y_b = part if y_b is None else y_b + part
    y_a = _dot(pm_ref[...], wa_ref[...])
    gate_a = jax.nn.sigmoid(ga_ref[...].astype(F32) + bias_ref[0:1, :])
    gate_b = jax.nn.sigmoid(gb_ref[...].astype(F32) + bias_ref[1:2, :])
    out_ref[...] = (gate_a * y_a + gate_b * y_b).astype(BF16)


def _mix(pm, o, z, gla_norm, w_a, w_b, bias, r_col, gate_col, tm):
    n = pm.shape[0]
    d = w_a.shape[1]
    val_w = o.shape[1]
    resident = dict(pipeline_mode=pl.Buffered(1))
    return pl.pallas_call(
        _mix_kernel,
        grid=(n // tm,),
        in_specs=[
            pl.BlockSpec((tm, pm.shape[1]), lambda i: (i, 0)),
            pl.BlockSpec((tm, val_w), lambda i: (i, 0)),
            pl.BlockSpec((tm, val_w), lambda i: (i, r_col // val_w)),
            pl.BlockSpec((1, gla_norm.shape[1]), lambda i: (0, 0)),
            pl.BlockSpec(w_a.shape, lambda i: (0, 0), **resident),
            pl.BlockSpec(w_b.shape, lambda i: (0, 0), **resident),
            pl.BlockSpec((tm, d), lambda i: (i, gate_col // d)),
            pl.BlockSpec((tm, d), lambda i: (i, gate_col // d + 1)),
            pl.BlockSpec((2, d), lambda i: (0, 0)),
        ],
        out_specs=pl.BlockSpec((tm, d), lambda i: (i, 0)),
        out_shape=jax.ShapeDtypeStruct((n, d), BF16),
        compiler_params=_params(("parallel",)),
        name="branch_mix",
    )(pm, o, z, gla_norm, w_a, w_b, z, z, bias)


def _outproj_kernel(m_ref, w_ref, x_ref, g_ref, o_ref, *, splits):
    rb = x_ref.shape[0] // splits
    for r0 in range(0, x_ref.shape[0], rb):
        rows = slice(r0, r0 + rb)
        y = _dot(m_ref[rows, :], w_ref[...])
        o_ref[rows, :] = x_ref[rows, :] + _rms(y, g_ref[...])


def _outproj(mixed, w_out, x2, g, tm):
    n, d = x2.shape
    return pl.pallas_call(
        functools.partial(_outproj_kernel, splits=2),
        grid=(n // tm,),
        in_specs=[
            pl.BlockSpec((tm, d), lambda i: (i, 0)),
            pl.BlockSpec((d, d), lambda i: (0, 0)),
            pl.BlockSpec((tm, d), lambda i: (i, 0)),
            pl.BlockSpec((1, d), lambda i: (0, 0)),
        ],
        out_specs=pl.BlockSpec((tm, d), lambda i: (i, 0)),
        out_shape=jax.ShapeDtypeStruct((n, d), F32),
        compiler_params=_params(("parallel",)),
        name="out_proj",
    )(mixed, w_out, x2, g)


def _ffn_kernel(x_ref, gpre_ref, wg_ref, wu_ref, wd_ref, gpost_ref, o_ref, h_ref, acc_ref, *, splits):
    j = pl.program_id(1)
    last = pl.num_programs(1) - 1
    rb = x_ref.shape[0] // splits
    blocks = [slice(r0, r0 + rb) for r0 in range(0, x_ref.shape[0], rb)]

    def hidden_tile(h):
        gate = _dot(h, wg_ref[...])
        up = _dot(h, wu_ref[...])
        act = (gate * jax.nn.sigmoid(gate) * up).astype(BF16)
        return _dot(act, wd_ref[...])

    @pl.when(j == 0)
    def _():
        for rows in blocks:
            h = _rms(x_ref[rows, :], gpre_ref[...]).astype(BF16)
            h_ref[rows, :] = h
            acc_ref[rows, :] = hidden_tile(h)

    @pl.when((j > 0) & (j < last))
    def _():
        acc_ref[...] += hidden_tile(h_ref[...])

    @pl.when(j == last)
    def _():
        for rows in blocks:
            f = acc_ref[rows, :] + hidden_tile(h_ref[rows, :])
            o_ref[rows, :] = x_ref[rows, :] + _rms(f, gpost_ref[...])


def _ffn(x1, g_pre, w_gate, w_up, w_down, g_post, tm, tf):
    n, d = x1.shape
    dff = w_gate.shape[1]
    nf = dff // tf
    ft = lambda i, j: jnp.where(i % 2 == 0, j, nf - 1 - j)
    return pl.pallas_call(
        functools.partial(_ffn_kernel, splits=2),
        grid=(n // tm, nf),
        in_specs=[
            pl.BlockSpec((tm, d), lambda i, j: (i, 0)),
            pl.BlockSpec((1, d), lambda i, j: (0, 0)),
            pl.BlockSpec((d, tf), lambda i, j: (0, ft(i, j))),
            pl.BlockSpec((d, tf), lambda i, j: (0, ft(i, j))),
            pl.BlockSpec((tf, d), lambda i, j: (ft(i, j), 0)),
            pl.BlockSpec((1, d), lambda i, j: (0, 0)),
        ],
        out_specs=pl.BlockSpec((tm, d), lambda i, j: (i, 0)),
        out_shape=jax.ShapeDtypeStruct((n, d), F32),
        scratch_shapes=[pltpu.VMEM((tm, d), BF16), pltpu.VMEM((tm, d), F32)],
        compiler_params=_params(("parallel", "arbitrary")),
        name="ffn",
    )(x1, g_pre, w_gate, w_up, w_down, g_post)


def _layer(x2, batch, seq, norm_mix_pre, w_in, w_gate_up, b_gate, w_pool, pool_scale, gla_norm,
           w_branch_a, w_branch_b, b_branch_gates, w_out, norm_mix_post,
           norm_ffn_pre, w_ffn_gate, w_ffn_up, w_ffn_down, norm_ffn_post):
    d = x2.shape[1]
    pool_w = w_branch_a.shape[0]
    key_w = w_gate_up.shape[1]
    val_w = w_branch_b.shape[0]
    rank = w_gate_up.shape[0]

    o_p, o_q, o_k = 0, pool_w, pool_w + key_w
    o_v = o_k + key_w
    o_g = o_v + val_w
    o_r = o_g + rank
    o_gate = o_r + val_w
    tn = 1024
    w_in_b = w_in.astype(BF16)
    w_hi = w_in_b[:, o_r:]
    n_lo = o_g // tn
    n_front = o_v // tn
    c_v, c_r, c_gate = 0, val_w, 2 * val_w
    c_p = c_gate + 2 * d
    c_q, c_k = c_p + pool_w, c_p + pool_w + key_w
    out_tile = lambda j: jnp.where(j < n_front, j + c_p // tn, j - n_front)
    w_glr = jnp.pad(w_in_b[:, o_g:o_r], ((0, 0), (0, LANES - rank)))
    w_gu = jnp.pad(w_gate_up, ((0, LANES - rank), (0, 0))).astype(BF16)

    z, glr = _inproj(x2, norm_mix_pre[None, :], w_in_b, w_hi, w_glr, n_lo, out_tile, tm=1024, tn=tn)
    pm = _pool(z, w_pool.astype(BF16), pool_scale[None, :], batch, seq, c_p // pool_w, tt=2048)
    o = _gla(z, glr, w_gu, b_gate[None, :], val_w, batch, seq, (c_q, c_k, c_v), tt=256)
    mixed = _mix(pm, o, z, gla_norm[None, :], w_branch_a.astype(BF16), w_branch_b.astype(BF16),
                 b_branch_gates, c_r, c_gate, tm=512)
    x1 = _outproj(mixed, w_out.astype(BF16), x2, norm_mix_post[None, :], tm=512)
    return _ffn(x1, norm_ffn_pre[None, :], w_ffn_gate.astype(BF16), w_ffn_up.astype(BF16),
                w_ffn_down.astype(BF16), norm_ffn_post[None, :], tm=512, tf=512)


def kernel(x, norm_mix_pre, w_in, w_gate_up, b_gate, w_pool, pool_scale, gla_norm, w_branch_a,
           w_branch_b, b_branch_gates, w_out, norm_mix_post, norm_ffn_pre, w_ffn_gate, w_ffn_up,
           w_ffn_down, norm_ffn_post):
    batch, seq, d = x.shape
    x2 = x.reshape(batch * seq, d)
    params = (norm_mix_pre, w_in, w_gate_up, b_gate, w_pool, pool_scale, gla_norm, w_branch_a,
              w_branch_b, b_branch_gates, w_out, norm_mix_post, norm_ffn_pre, w_ffn_gate,
              w_ffn_up, w_ffn_down, norm_ffn_post)
    for layer in range(norm_mix_pre.shape[0]):
        x2 = _layer(x2, batch, seq, *(p[layer] for p in params))
    return x2.reshape(batch, seq, d)
```

```python
import functools

import jax
import jax.numpy as jnp
from jax import lax
from jax.experimental import pallas as pl
from jax.experimental.pallas import tpu as pltpu

F32 = jnp.float32
BF16 = jnp.bfloat16

EPS = 1e-6
POOL_WINDOWS = (2, 4, 8, 16)
POOL_HALO = 16
GLA_HEADS = 4
GLA_GATE_TAU = 16.0
GLA_CHUNK = 128
LOG2E = 1.4426950408889634
SUBLANES = 8
LANES = 128
VMEM_LIMIT = 60 * 1024 * 1024


def _params(sem):
    return pltpu.CompilerParams(dimension_semantics=sem, vmem_limit_bytes=VMEM_LIMIT)


def _rms(x, g):
    ms = jnp.mean(x * x, axis=-1, keepdims=True)
    return x * lax.rsqrt(ms + EPS) * g


def _dot(a, b):
    return jnp.dot(a, b, preferred_element_type=F32)


def _inproj_kernel(x_ref, g_ref, wlo_ref, whi_ref, wglr_ref, z_ref, glr_ref, h_ref, *, n_lo, splits):
    j = pl.program_id(1)

    @pl.when(j == 0)
    def _():
        rb = x_ref.shape[0] // splits
        for r0 in range(0, x_ref.shape[0], rb):
            rows = slice(r0, r0 + rb)
            hb = _rms(x_ref[rows, :], g_ref[...]).astype(BF16)
            h_ref[rows, :] = hb
            glr_ref[rows, :] = _dot(hb, wglr_ref[...])
            z_ref[rows, :] = _dot(hb, wlo_ref[...]).astype(BF16)

    @pl.when((j > 0) & (j < n_lo))
    def _():
        z_ref[...] = _dot(h_ref[...], wlo_ref[...]).astype(BF16)

    @pl.when(j >= n_lo)
    def _():
        z_ref[...] = _dot(h_ref[...], whi_ref[...]).astype(BF16)


def _inproj(x2, g, w_lo, w_hi, w_glr, n_lo, out_tile, tm, tn):
    n, d = x2.shape
    n_hi = w_hi.shape[1] // tn

    def lo_tile(i, j):
        t = jnp.minimum(j, n_lo - 1)
        return jnp.where(i % 2 == 0, t, n_lo - 1 - t)

    def hi_tile(i, j):
        t = jnp.maximum(j - n_lo, 0)
        return jnp.where(i % 2 == 0, t, n_hi - 1 - t)

    def z_tile(i, j):
        return out_tile(jnp.where(j < n_lo, lo_tile(i, j), n_lo + hi_tile(i, j)))

    return pl.pallas_call(
        functools.partial(_inproj_kernel, n_lo=n_lo, splits=4),
        grid=(n // tm, n_lo + n_hi),
        in_specs=[
            pl.BlockSpec((tm, d), lambda i, j: (i, 0)),
            pl.BlockSpec((1, d), lambda i, j: (0, 0)),
            pl.BlockSpec((d, tn), lambda i, j: (0, lo_tile(i, j))),
            pl.BlockSpec((d, tn), lambda i, j: (0, hi_tile(i, j))),
            pl.BlockSpec((d, LANES), lambda i, j: (0, 0)),
        ],
        out_specs=[
            pl.BlockSpec((tm, tn), lambda i, j: (i, z_tile(i, j))),
            pl.BlockSpec((tm, LANES), lambda i, j: (i, 0)),
        ],
        out_shape=[
            jax.ShapeDtypeStruct((n, (n_lo + n_hi) * tn), BF16),
            jax.ShapeDtypeStruct((n, LANES), F32),
        ],
        scratch_shapes=[pltpu.VMEM((tm, d), BF16)],
        compiler_params=_params(("parallel", "arbitrary")),
        name="in_proj",
    )(x2, g, w_lo, w_hi, w_glr)


def _pool_kernel(p_ref, wp_ref, sc_ref, o_ref, carry_ref, *, tt):
    t = pl.program_id(1)

    @pl.when(t == 0)
    def _():
        carry_ref[...] = jnp.zeros_like(carry_ref)

    p = p_ref[...].astype(F32)
    ext = jnp.concatenate([carry_ref[...], p], axis=0)
    carry_ref[...] = p[tt - POOL_HALO:, :]
    pos = t * tt + lax.broadcasted_iota(jnp.int32, (tt, 1), 0)
    gd = wp_ref.shape[1]
    outs = []
    for g, w in enumerate(POOL_WINDOWS):
        s = ext[:, g * gd:(g + 1) * gd]
        shift = 1
        while shift < w:
            s = s + pltpu.roll(s, shift, 0)
            shift *= 2
        count = jnp.minimum(pos + 1, w).astype(F32)
        d = s[POOL_HALO:, :] / count - p[:, g * gd:(g + 1) * gd]
        y = _dot(d.astype(BF16), wp_ref[g])
        outs.append(y * sc_ref[:, g * gd:(g + 1) * gd])
    o_ref[...] = jnp.concatenate(outs, axis=1).astype(BF16)


def _pool(z, w_pool, scale, batch, seq, col_block, tt):
    width = w_pool.shape[0] * w_pool.shape[1]
    nt = seq // tt
    return pl.pallas_call(
        functools.partial(_pool_kernel, tt=tt),
        grid=(batch, nt),
        in_specs=[
            pl.BlockSpec((tt, width), lambda b, t: (b * nt + t, col_block)),
            pl.BlockSpec(w_pool.shape, lambda b, t: (0, 0, 0)),
            pl.BlockSpec((1, width), lambda b, t: (0, 0)),
        ],
        out_specs=pl.BlockSpec((tt, width), lambda b, t: (b * nt + t, 0)),
        out_shape=jax.ShapeDtypeStruct((batch * seq, width), BF16),
        scratch_shapes=[pltpu.VMEM((POOL_HALO, width), F32)],
        compiler_params=_params(("parallel", "arbitrary")),
        name="pool_mixer",
    )(z, w_pool, scale)


_NT = (((1,), (1,)), ((), ()))
_TN = (((0,), (0,)), ((), ()))


def _store_tiles(ref, x):
    for j in range(ref.shape[0]):
        ref[j] = x[:, j * LANES:(j + 1) * LANES]


def _rows(ref, r0, n):
    return jnp.concatenate([ref[j, pl.ds(r0, n), :] for j in range(ref.shape[0])], axis=1)


def _bcast_row(ref, r, n):
    return jnp.concatenate([ref[j, pl.ds(r, n, stride=0), :] for j in range(ref.shape[0])], axis=1)


def _gla_kernel(q_ref, k_ref, v_ref, glr_ref, wgu_ref, bg_ref,
                o_ref, s_ref, g_scr, q_scr, k_scr, *, tt, scale):
    c = GLA_CHUNK
    kw = q_ref.shape[1]
    dk = kw // GLA_HEADS
    dv = v_ref.shape[1] // GLA_HEADS
    heads = range(GLA_HEADS)

    @pl.when(pl.program_id(1) == 0)
    def _():
        s_ref[...] = jnp.zeros_like(s_ref)

    row = lax.broadcasted_iota(jnp.int32, (c, c), 0)
    col = lax.broadcasted_iota(jnp.int32, (c, c), 1)
    tri = (row >= col).astype(BF16)
    sel_r = lax.broadcasted_iota(jnp.int32, (LANES, c), 0)
    sel_c = lax.broadcasted_iota(jnp.int32, (LANES, c), 1)
    sel = (sel_r == (sel_c % SUBLANES)).astype(BF16)
    lane = lax.broadcasted_iota(jnp.int32, (SUBLANES, LANES), 1)
    sub = lax.broadcasted_iota(jnp.int32, (SUBLANES, LANES), 0)
    slot_mask = [(lane == s) & (sub >= s) for s in range(SUBLANES)]
    in_group = ((row // SUBLANES) == (col // SUBLANES)) & (row >= col)
    level_sizes = []
    b = c // 2
    while b >= 2 * SUBLANES:
        level_sizes.append(b)
        b //= 2
    same_block = [(row // b) == (col // b) for b in level_sizes]

    for ci in range(tt // c):
        rows = slice(ci * c, (ci + 1) * c)
        u = _dot(glr_ref[rows, :].astype(BF16), wgu_ref[...]) + bg_ref[...]
        la = (jnp.minimum(u, 0.0) - jnp.log(1.0 + jnp.exp(-jnp.abs(u)))) * (LOG2E / GLA_GATE_TAU)
        la_hi = la.astype(BF16)
        la_lo = (la - la_hi.astype(F32)).astype(BF16)
        gcum = _dot(tri, la_hi) + _dot(tri, la_lo)
        _store_tiles(g_scr, gcum)
        _store_tiles(q_scr, q_ref[rows, :].astype(F32) * scale)
        _store_tiles(k_scr, k_ref[rows, :].astype(F32))

        def level(b):
            half = b // 2
            qs, ks = [], []
            for s in range(0, c, b):
                gref = _bcast_row(g_scr, s + half, half)
                ks.append((_rows(k_scr, s, half) * jnp.exp2(gref - _rows(g_scr, s, half))).astype(BF16))
                ks.append(jnp.zeros((half, kw), BF16))
                qs.append(jnp.zeros((half, kw), BF16))
                qs.append((_rows(q_scr, s + half, half)
                           * jnp.exp2(_rows(g_scr, s + half, half) - gref)).astype(BF16))
            qb, kb = jnp.concatenate(qs, 0), jnp.concatenate(ks, 0)
            return [lax.dot_general(qb[:, h * dk:(h + 1) * dk], kb[:, h * dk:(h + 1) * dk], _NT,
                                    preferred_element_type=F32) for h in heads]

        a = level(c)
        for b, mask in zip(level_sizes, same_block):
            a = [jnp.where(mask, new, old) for new, old in zip(level(b), a)]

        groups = [[] for _ in heads]
        for g0 in range(0, c, SUBLANES):
            qg, gg = _rows(q_scr, g0, SUBLANES), _rows(g_scr, g0, SUBLANES)
            slots = [jnp.zeros((SUBLANES, LANES), F32) for _ in heads]
            for s in range(SUBLANES):
                p = qg * _bcast_row(k_scr, g0 + s, SUBLANES) * jnp.exp2(gg - _bcast_row(g_scr, g0 + s, SUBLANES))
                for h in heads:
                    score = jnp.sum(p[:, h * dk:(h + 1) * dk], axis=1, keepdims=True)
                    slots[h] = jnp.where(slot_mask[s], score, slots[h])
            for h in heads:
                groups[h].append(slots[h])
        a = [jnp.where(in_group, _dot(jnp.concatenate(groups[h], 0).astype(BF16), sel), a[h]).astype(BF16)
             for h in heads]

        gcum = _rows(g_scr, 0, c)
        g_last8 = _bcast_row(g_scr, c - 1, SUBLANES)
        g_last = jnp.concatenate([g_last8] * (c // SUBLANES), 0)
        qg = (_rows(q_scr, 0, c) * jnp.exp2(gcum)).astype(BF16)
        kd = (_rows(k_scr, 0, c) * jnp.exp2(g_last - gcum)).astype(BF16)
        decay_t = jnp.concatenate([jnp.exp2(g_last8)] * (LANES // SUBLANES), 0)
        for h in heads:
            kc = slice(h * dk, (h + 1) * dk)
            vc = slice(h * dv, (h + 1) * dv)
            v = v_ref[rows, vc]
            s_old = s_ref[h]
            o = _dot(qg[:, kc], s_old.astype(BF16)) + _dot(a[h], v)
            decay = decay_t[:, kc].T
            decay = jnp.concatenate([decay] * (dv // LANES), axis=1)
            s_ref[h] = decay * s_old + lax.dot_general(kd[:, kc], v, _TN, preferred_element_type=F32)
            o_ref[rows, vc] = o.astype(BF16)


def _gla(z, glr, w_gate_up, b_gate, val_w, batch, seq, cols, tt):
    key_w = w_gate_up.shape[1]
    dk = key_w // GLA_HEADS
    dv = val_w // GLA_HEADS
    nt = seq // tt
    q0, k0, v0 = cols
    tiles = key_w // LANES
    return pl.pallas_call(
        functools.partial(_gla_kernel, tt=tt, scale=dk ** -0.5),
        grid=(batch, nt),
        in_specs=[
            pl.BlockSpec((tt, key_w), lambda b, t: (b * nt + t, q0 // key_w)),
            pl.BlockSpec((tt, key_w), lambda b, t: (b * nt + t, k0 // key_w)),
            pl.BlockSpec((tt, val_w), lambda b, t: (b * nt + t, v0 // val_w)),
            pl.BlockSpec((tt, LANES), lambda b, t: (b * nt + t, 0)),
            pl.BlockSpec((LANES, key_w), lambda b, t: (0, 0)),
            pl.BlockSpec((1, key_w), lambda b, t: (0, 0)),
        ],
        out_specs=pl.BlockSpec((tt, val_w), lambda b, t: (b * nt + t, 0)),
        out_shape=jax.ShapeDtypeStruct((batch * seq, val_w), BF16),
        scratch_shapes=[
            pltpu.VMEM((GLA_HEADS, dk, dv), F32),
            pltpu.VMEM((tiles, GLA_CHUNK, LANES), F32),
            pltpu.VMEM((tiles, GLA_CHUNK, LANES), F32),
            pltpu.VMEM((tiles, GLA_CHUNK, LANES), F32),
        ],
        compiler_params=_params(("parallel", "arbitrary")),
        name="gla",
    )(z, z, z, glr, w_gate_up, b_gate)


def _mix_kernel(pm_ref, o_ref, r_ref, gn_ref, wa_ref, wb_ref, ga_ref, gb_ref, bias_ref,
                out_ref):
    dv = gn_ref.shape[1]
    y_b = None
    for h in range(o_ref.shape[1] // dv):
        vc = slice(h * dv, (h + 1) * dv)
        r = r_ref[:, vc].astype(F32)
        on = (_rms(o_ref[:, vc].astype(F32), gn_ref[...]) * (r * jax.nn.sigmoid(r))).astype(BF16)
        part = _dot(on, wb_ref[vc, :])
        y_b = part if y_b is None else y_b + part
    y_a = _dot(pm_ref[...], wa_ref[...])
    gate_a = jax.nn.sigmoid(ga_ref[...].astype(F32) + bias_ref[0:1, :])
    gate_b = jax.nn.sigmoid(gb_ref[...].astype(F32) + bias_ref[1:2, :])
    out_ref[...] = (gate_a * y_a + gate_b * y_b).astype(BF16)


def _mix(pm, o, z, gla_norm, w_a, w_b, bias, r_col, gate_col, tm):
    n = pm.shape[0]
    d = w_a.shape[1]
    val_w = o.shape[1]
    resident = dict(pipeline_mode=pl.Buffered(1))
    return pl.pallas_call(
        _mix_kernel,
        grid=(n // tm,),
        in_specs=[
            pl.BlockSpec((tm, pm.shape[1]), lambda i: (i, 0)),
            pl.BlockSpec((tm, val_w), lambda i: (i, 0)),
            pl.BlockSpec((tm, val_w), lambda i: (i, r_col // val_w)),
            pl.BlockSpec((1, gla_norm.shape[1]), lambda i: (0, 0)),
            pl.BlockSpec(w_a.shape, lambda i: (0, 0), **resident),
            pl.BlockSpec(w_b.shape, lambda i: (0, 0), **resident),
            pl.BlockSpec((tm, d), lambda i: (i, gate_col // d)),
            pl.BlockSpec((tm, d), lambda i: (i, gate_col // d + 1)),
            pl.BlockSpec((2, d), lambda i: (0, 0)),
        ],
        out_specs=pl.BlockSpec((tm, d), lambda i: (i, 0)),
        out_shape=jax.ShapeDtypeStruct((n, d), BF16),
        compiler_params=_params(("parallel",)),
        name="branch_mix",
    )(pm, o, z, gla_norm, w_a, w_b, z, z, bias)


def _outproj_kernel(m_ref, w_ref, x_ref, g_ref, o_ref, *, splits):
    rb = x_ref.shape[0] // splits
    for r0 in range(0, x_ref.shape[0], rb):
        rows = slice(r0, r0 + rb)
        y = _dot(m_ref[rows, :], w_ref[...])
        o_ref[rows, :] = x_ref[rows, :] + _rms(y, g_ref[...])


def _outproj(mixed, w_out, x2, g, tm):
    n, d = x2.shape
    return pl.pallas_call(
        functools.partial(_outproj_kernel, splits=1),
        grid=(n // tm,),
        in_specs=[
            pl.BlockSpec((tm, d), lambda i: (i, 0)),
            pl.BlockSpec((d, d), lambda i: (0, 0)),
            pl.BlockSpec((tm, d), lambda i: (i, 0)),
            pl.BlockSpec((1, d), lambda i: (0, 0)),
        ],
        out_specs=pl.BlockSpec((tm, d), lambda i: (i, 0)),
        out_shape=jax.ShapeDtypeStruct((n, d), F32),
        compiler_params=_params(("parallel",)),
        name="out_proj",
    )(mixed, w_out, x2, g)


def _ffn_kernel(x_ref, gpre_ref, wg_ref, wu_ref, wd_ref, gpost_ref, o_ref, h_ref, acc_ref, *, splits):
    j = pl.program_id(1)
    last = pl.num_programs(1) - 1
    rb = x_ref.shape[0] // splits
    blocks = [slice(r0, r0 + rb) for r0 in range(0, x_ref.shape[0], rb)]

    def hidden_tile(h):
        gate = _dot(h, wg_ref[...])
        up = _dot(h, wu_ref[...])
        act = (gate * jax.nn.sigmoid(gate) * up).astype(BF16)
        return _dot(act, wd_ref[...])

    @pl.when(j == 0)
    def _():
        for rows in blocks:
            h = _rms(x_ref[rows, :], gpre_ref[...]).astype(BF16)
            h_ref[rows, :] = h
            acc_ref[rows, :] = hidden_tile(h)

    @pl.when((j > 0) & (j < last))
    def _():
        acc_ref[...] += hidden_tile(h_ref[...])

    @pl.when(j == last)
    def _():
        for rows in blocks:
            f = acc_ref[rows, :] + hidden_tile(h_ref[rows, :])
            o_ref[rows, :] = x_ref[rows, :] + _rms(f, gpost_ref[...])


def _ffn(x1, g_pre, w_gate, w_up, w_down, g_post, tm, tf):
    n, d = x1.shape
    dff = w_gate.shape[1]
    nf = dff // tf
    ft = lambda i, j: jnp.where(i % 2 == 0, j, nf - 1 - j)
    return pl.pallas_call(
        functools.partial(_ffn_kernel, splits=2),
        grid=(n // tm, nf),
        in_specs=[
            pl.BlockSpec((tm, d), lambda i, j: (i, 0)),
            pl.BlockSpec((1, d), lambda i, j: (0, 0)),
            pl.BlockSpec((d, tf), lambda i, j: (0, ft(i, j))),
            pl.BlockSpec((d, tf), lambda i, j: (0, ft(i, j))),
            pl.BlockSpec((tf, d), lambda i, j: (ft(i, j), 0)),
            pl.BlockSpec((1, d), lambda i, j: (0, 0)),
        ],
        out_specs=pl.BlockSpec((tm, d), lambda i, j: (i, 0)),
        out_shape=jax.ShapeDtypeStruct((n, d), F32),
        scratch_shapes=[pltpu.VMEM((tm, d), BF16), pltpu.VMEM((tm, d), F32)],
        compiler_params=_params(("parallel", "arbitrary")),
        name="ffn",
    )(x1, g_pre, w_gate, w_up, w_down, g_post)


def _layer(x2, batch, seq, norm_mix_pre, w_in, w_gate_up, b_gate, w_pool, pool_scale, gla_norm,
           w_branch_a, w_branch_b, b_branch_gates, w_out, norm_mix_post,
           norm_ffn_pre, w_ffn_gate, w_ffn_up, w_ffn_down, norm_ffn_post):
    d = x2.shape[1]
    pool_w = w_branch_a.shape[0]
    key_w = w_gate_up.shape[1]
    val_w = w_branch_b.shape[0]
    rank = w_gate_up.shape[0]

    o_p, o_q, o_k = 0, pool_w, pool_w + key_w
    o_v = o_k + key_w
    o_g = o_v + val_w
    o_r = o_g + rank
    o_gate = o_r + val_w
    tn = 1024
    w_in_b = w_in.astype(BF16)
    w_hi = w_in_b[:, o_r:]
    n_lo = o_g // tn
    n_front = o_v // tn
    c_v, c_r, c_gate = 0, val_w, 2 * val_w
    c_p = c_gate + 2 * d
    c_q, c_k = c_p + pool_w, c_p + pool_w + key_w
    out_tile = lambda j: jnp.where(j < n_front, j + c_p // tn, j - n_front)
    w_glr = jnp.pad(w_in_b[:, o_g:o_r], ((0, 0), (0, LANES - rank)))
    w_gu = jnp.pad(w_gate_up, ((0, LANES - rank), (0, 0))).astype(BF16)

    z, glr = _inproj(x2, norm_mix_pre[None, :], w_in_b, w_hi, w_glr, n_lo, out_tile, tm=1024, tn=tn)
    pm = _pool(z, w_pool.astype(BF16), pool_scale[None, :], batch, seq, c_p // pool_w, tt=2048)
    o = _gla(z, glr, w_gu, b_gate[None, :], val_w, batch, seq, (c_q, c_k, c_v), tt=256)
    mixed = _mix(pm, o, z, gla_norm[None, :], w_branch_a.astype(BF16), w_branch_b.astype(BF16),
                 b_branch_gates, c_r, c_gate, tm=512)
    x1 = _outproj(mixed, w_out.astype(BF16), x2, norm_mix_post[None, :], tm=512)
    return _ffn(x1, norm_ffn_pre[None, :], w_ffn_gate.astype(BF16), w_ffn_up.astype(BF16),
                w_ffn_down.astype(BF16), norm_ffn_post[None, :], tm=1024, tf=256)


def kernel(x, norm_mix_pre, w_in, w_gate_up, b_gate, w_pool, pool_scale, gla_norm, w_branch_a,
           w_branch_b, b_branch_gates, w_out, norm_mix_post, norm_ffn_pre, w_ffn_gate, w_ffn_up,
           w_ffn_down, norm_ffn_post):
    batch, seq, d = x.shape
    x2 = x.reshape(batch * seq, d)
    params = (norm_mix_pre, w_in, w_gate_up, b_gate, w_pool, pool_scale, gla_norm, w_branch_a,
              w_branch_b, b_branch_gates, w_out, norm_mix_post, norm_ffn_pre, w_ffn_gate,
              w_ffn_up, w_ffn_down, norm_ffn_post)
    for layer in range(norm_mix_pre.shape[0]):
        x2 = _layer(x2, batch, seq, *(p[layer] for p in params))
    return x2.reshape(batch, seq, d)
```

```python
import functools

import jax
import jax.numpy as jnp
from jax import lax
from jax.experimental import pallas as pl
from jax.experimental.pallas import tpu as pltpu

F32 = jnp.float32
BF16 = jnp.bfloat16

EPS = 1e-6
POOL_WINDOWS = (2, 4, 8, 16)
POOL_HALO = 16
GLA_HEADS = 4
GLA_GATE_TAU = 16.0
GLA_CHUNK = 128
LOG2E = 1.4426950408889634
SUBLANES = 8
LANES = 128
VMEM_LIMIT = 62 * 1024 * 1024


def _params(sem):
    return pltpu.CompilerParams(dimension_semantics=sem, vmem_limit_bytes=VMEM_LIMIT)


def _rms(x, g):
    ms = jnp.mean(x * x, axis=-1, keepdims=True)
    return x * lax.rsqrt(ms + EPS) * g


def _dot(a, b):
    return jnp.dot(a, b, preferred_element_type=F32)


def _inproj_kernel(x_ref, g_ref, wlo_ref, whi_ref, wglr_ref, z_ref, glr_ref, h_ref, *, n_lo, splits):
    j = pl.program_id(1)

    @pl.when(j == 0)
    def _():
        rb = x_ref.shape[0] // splits
        for r0 in range(0, x_ref.shape[0], rb):
            rows = slice(r0, r0 + rb)
            hb = _rms(x_ref[rows, :], g_ref[...]).astype(BF16)
            h_ref[rows, :] = hb
            glr_ref[rows, :] = _dot(hb, wglr_ref[...])
            z_ref[rows, :] = _dot(hb, wlo_ref[...]).astype(BF16)

    @pl.when((j > 0) & (j < n_lo))
    def _():
        z_ref[...] = _dot(h_ref[...], wlo_ref[...]).astype(BF16)

    @pl.when(j >= n_lo)
    def _():
        z_ref[...] = _dot(h_ref[...], whi_ref[...]).astype(BF16)


def _inproj(x2, g, w_lo, w_hi, w_glr, n_lo, out_tile, tm, tn):
    n, d = x2.shape
    n_hi = w_hi.shape[1] // tn

    def lo_tile(i, j):
        t = jnp.minimum(j, n_lo - 1)
        return jnp.where(i % 2 == 0, t, n_lo - 1 - t)

    def hi_tile(i, j):
        t = jnp.maximum(j - n_lo, 0)
        return jnp.where(i % 2 == 0, t, n_hi - 1 - t)

    def z_tile(i, j):
        return out_tile(jnp.where(j < n_lo, lo_tile(i, j), n_lo + hi_tile(i, j)))

    return pl.pallas_call(
        functools.partial(_inproj_kernel, n_lo=n_lo, splits=4),
        grid=(n // tm, n_lo + n_hi),
        in_specs=[
            pl.BlockSpec((tm, d), lambda i, j: (i, 0)),
            pl.BlockSpec((1, d), lambda i, j: (0, 0)),
            pl.BlockSpec((d, tn), lambda i, j: (0, lo_tile(i, j))),
            pl.BlockSpec((d, tn), lambda i, j: (0, hi_tile(i, j))),
            pl.BlockSpec((d, LANES), lambda i, j: (0, 0)),
        ],
        out_specs=[
            pl.BlockSpec((tm, tn), lambda i, j: (i, z_tile(i, j))),
            pl.BlockSpec((tm, LANES), lambda i, j: (i, 0)),
        ],
        out_shape=[
            jax.ShapeDtypeStruct((n, (n_lo + n_hi) * tn), BF16),
            jax.ShapeDtypeStruct((n, LANES), F32),
        ],
        scratch_shapes=[pltpu.VMEM((tm, d), BF16)],
        compiler_params=_params(("parallel", "arbitrary")),
        name="in_proj",
    )(x2, g, w_lo, w_hi, w_glr)


def _pool_kernel(p_ref, wp_ref, sc_ref, o_ref, carry_ref, *, tt):
    t = pl.program_id(1)

    @pl.when(t == 0)
    def _():
        carry_ref[...] = jnp.zeros_like(carry_ref)

    p = p_ref[...].astype(F32)
    ext = jnp.concatenate([carry_ref[...], p], axis=0)
    carry_ref[...] = p[tt - POOL_HALO:, :]
    pos = t * tt + lax.broadcasted_iota(jnp.int32, (tt, 1), 0)
    gd = wp_ref.shape[1]
    outs = []
    for g, w in enumerate(POOL_WINDOWS):
        s = ext[:, g * gd:(g + 1) * gd]
        shift = 1
        while shift < w:
            s = s + pltpu.roll(s, shift, 0)
            shift *= 2
        count = jnp.minimum(pos + 1, w).astype(F32)
        d = s[POOL_HALO:, :] / count - p[:, g * gd:(g + 1) * gd]
        y = _dot(d.astype(BF16), wp_ref[g])
        outs.append(y * sc_ref[:, g * gd:(g + 1) * gd])
    o_ref[...] = jnp.concatenate(outs, axis=1).astype(BF16)


def _pool(z, w_pool, scale, batch, seq, col_block, tt):
    width = w_pool.shape[0] * w_pool.shape[1]
    nt = seq // tt
    return pl.pallas_call(
        functools.partial(_pool_kernel, tt=tt),
        grid=(batch, nt),
        in_specs=[
            pl.BlockSpec((tt, width), lambda b, t: (b * nt + t, col_block)),
            pl.BlockSpec(w_pool.shape, lambda b, t: (0, 0, 0)),
            pl.BlockSpec((1, width), lambda b, t: (0, 0)),
        ],
        out_specs=pl.BlockSpec((tt, width), lambda b, t: (b * nt + t, 0)),
        out_shape=jax.ShapeDtypeStruct((batch * seq, width), BF16),
        scratch_shapes=[pltpu.VMEM((POOL_HALO, width), F32)],
        compiler_params=_params(("parallel", "arbitrary")),
        name="pool_mixer",
    )(z, w_pool, scale)


_NT = (((1,), (1,)), ((), ()))
_TN = (((0,), (0,)), ((), ()))


def _store_tiles(ref, x):
    for j in range(ref.shape[0]):
        ref[j] = x[:, j * LANES:(j + 1) * LANES]


def _rows(ref, r0, n):
    return jnp.concatenate([ref[j, pl.ds(r0, n), :] for j in range(ref.shape[0])], axis=1)


def _bcast_row(ref, r, n):
    return jnp.concatenate([ref[j, pl.ds(r, n, stride=0), :] for j in range(ref.shape[0])], axis=1)


def _gla_kernel(q_ref, k_ref, v_ref, glr_ref, wgu_ref, bg_ref,
                o_ref, s_ref, g_scr, q_scr, k_scr, *, tt, scale):
    c = GLA_CHUNK
    kw = q_ref.shape[1]
    dk = kw // GLA_HEADS
    dv = v_ref.shape[1] // GLA_HEADS
    heads = range(GLA_HEADS)

    @pl.when(pl.program_id(1) == 0)
    def _():
        s_ref[...] = jnp.zeros_like(s_ref)

    row = lax.broadcasted_iota(jnp.int32, (c, c), 0)
    col = lax.broadcasted_iota(jnp.int32, (c, c), 1)
    tri = (row >= col).astype(BF16)
    sel_r = lax.broadcasted_iota(jnp.int32, (LANES, c), 0)
    sel_c = lax.broadcasted_iota(jnp.int32, (LANES, c), 1)
    sel = (sel_r == (sel_c % SUBLANES)).astype(BF16)
    lane = lax.broadcasted_iota(jnp.int32, (SUBLANES, LANES), 1)
    sub = lax.broadcasted_iota(jnp.int32, (SUBLANES, LANES), 0)
    slot_mask = [(lane == s) & (sub >= s) for s in range(SUBLANES)]
    in_group = ((row // SUBLANES) == (col // SUBLANES)) & (row >= col)
    level_sizes = []
    b = c // 2
    while b >= 2 * SUBLANES:
        level_sizes.append(b)
        b //= 2
    same_block = [(row // b) == (col // b) for b in level_sizes]

    for ci in range(tt // c):
        rows = slice(ci * c, (ci + 1) * c)
        u = _dot(glr_ref[rows, :].astype(BF16), wgu_ref[...]) + bg_ref[...]
        la = (jnp.minimum(u, 0.0) - jnp.log(1.0 + jnp.exp(-jnp.abs(u)))) * (LOG2E / GLA_GATE_TAU)
        la_hi = la.astype(BF16)
        la_lo = (la - la_hi.astype(F32)).astype(BF16)
        gcum = _dot(tri, la_hi) + _dot(tri, la_lo)
        _store_tiles(g_scr, gcum)
        _store_tiles(q_scr, q_ref[rows, :].astype(F32) * scale)
        _store_tiles(k_scr, k_ref[rows, :].astype(F32))

        def level(b):
            half = b // 2
            qs, ks = [], []
            for s in range(0, c, b):
                gref = _bcast_row(g_scr, s + half, half)
                ks.append((_rows(k_scr, s, half) * jnp.exp2(gref - _rows(g_scr, s, half))).astype(BF16))
                ks.append(jnp.zeros((half, kw), BF16))
                qs.append(jnp.zeros((half, kw), BF16))
                qs.append((_rows(q_scr, s + half, half)
                           * jnp.exp2(_rows(g_scr, s + half, half) - gref)).astype(BF16))
            qb, kb = jnp.concatenate(qs, 0), jnp.concatenate(ks, 0)
            return [lax.dot_general(qb[:, h * dk:(h + 1) * dk], kb[:, h * dk:(h + 1) * dk], _NT,
                                    preferred_element_type=F32) for h in heads]

        a = level(c)
        for b, mask in zip(level_sizes, same_block):
            a = [jnp.where(mask, new, old) for new, old in zip(level(b), a)]

        groups = [[] for _ in heads]
        for g0 in range(0, c, SUBLANES):
            qg, gg = _rows(q_scr, g0, SUBLANES), _rows(g_scr, g0, SUBLANES)
            slots = [jnp.zeros((SUBLANES, LANES), F32) for _ in heads]
            for s in range(SUBLANES):
                p = qg * _bcast_row(k_scr, g0 + s, SUBLANES) * jnp.exp2(gg - _bcast_row(g_scr, g0 + s, SUBLANES))
                for h in heads:
                    score = jnp.sum(p[:, h * dk:(h + 1) * dk], axis=1, keepdims=True)
                    slots[h] = jnp.where(slot_mask[s], score, slots[h])
            for h in heads:
                groups[h].append(slots[h])
        a = [jnp.where(in_group, _dot(jnp.concatenate(groups[h], 0).astype(BF16), sel), a[h]).astype(BF16)
             for h in heads]

        gcum = _rows(g_scr, 0, c)
        g_last8 = _bcast_row(g_scr, c - 1, SUBLANES)
        g_last = jnp.concatenate([g_last8] * (c // SUBLANES), 0)
        qg = (_rows(q_scr, 0, c) * jnp.exp2(gcum)).astype(BF16)
        kd = (_rows(k_scr, 0, c) * jnp.exp2(g_last - gcum)).astype(BF16)
        decay_t = jnp.concatenate([jnp.exp2(g_last8)] * (LANES // SUBLANES), 0)
        for h in heads:
            kc = slice(h * dk, (h + 1) * dk)
            vc = slice(h * dv, (h + 1) * dv)
            v = v_ref[rows, vc]
            s_old = s_ref[h]
            o = _dot(qg[:, kc], s_old.astype(BF16)) + _dot(a[h], v)
            decay = decay_t[:, kc].T
            decay = jnp.concatenate([decay] * (dv // LANES), axis=1)
            s_ref[h] = decay * s_old + lax.dot_general(kd[:, kc], v, _TN, preferred_element_type=F32)
            o_ref[rows, vc] = o.astype(BF16)


def _gla(z, glr, w_gate_up, b_gate, val_w, batch, seq, cols, tt):
    key_w = w_gate_up.shape[1]
    dk = key_w // GLA_HEADS
    dv = val_w // GLA_HEADS
    nt = seq // tt
    q0, k0, v0 = cols
    tiles = key_w // LANES
    return pl.pallas_call(
        functools.partial(_gla_kernel, tt=tt, scale=dk ** -0.5),
        grid=(batch, nt),
        in_specs=[
            pl.BlockSpec((tt, key_w), lambda b, t: (b * nt + t, q0 // key_w)),
            pl.BlockSpec((tt, key_w), lambda b, t: (b * nt + t, k0 // key_w)),
            pl.BlockSpec((tt, val_w), lambda b, t: (b * nt + t, v0 // val_w)),
            pl.BlockSpec((tt, LANES), lambda b, t: (b * nt + t, 0)),
            pl.BlockSpec((LANES, key_w), lambda b, t: (0, 0)),
            pl.BlockSpec((1, key_w), lambda b, t: (0, 0)),
        ],
        out_specs=pl.BlockSpec((tt, val_w), lambda b, t: (b * nt + t, 0)),
        out_shape=jax.ShapeDtypeStruct((batch * seq, val_w), BF16),
        scratch_shapes=[
            pltpu.VMEM((GLA_HEADS, dk, dv), F32),
            pltpu.VMEM((tiles, GLA_CHUNK, LANES), F32),
            pltpu.VMEM((tiles, GLA_CHUNK, LANES), F32),
            pltpu.VMEM((tiles, GLA_CHUNK, LANES), F32),
        ],
        compiler_params=_params(("parallel", "arbitrary")),
        name="gla",
    )(z, z, z, glr, w_gate_up, b_gate)


def _mix_kernel(pm_ref, o_ref, r_ref, gn_ref, wa_ref, wb_ref, ga_ref, gb_ref, bias_ref,
                out_ref):
    dv = gn_ref.shape[1]
    y_b = None
    for h in range(o_ref.shape[1] // dv):
        vc = slice(h * dv, (h + 1) * dv)
        r = r_ref[:, vc].astype(F32)
        on = (_rms(o_ref[:, vc].astype(F32), gn_ref[...]) * (r * jax.nn.sigmoid(r))).astype(BF16)
        part = _dot(on, wb_ref[vc, :])
        y_b = part if y_b is None else y_b + part
    y_a = _dot(pm_ref[...], wa_ref[...])
    gate_a = jax.nn.sigmoid(ga_ref[...].astype(F32) + bias_ref[0:1, :])
    gate_b = jax.nn.sigmoid(gb_ref[...].astype(F32) + bias_ref[1:2, :])
    out_ref[...] = (gate_a * y_a + gate_b * y_b).astype(BF16)


def _mix(pm, o, z, gla_norm, w_a, w_b, bias, r_col, gate_col, tm):
    n = pm.shape[0]
    d = w_a.shape[1]
    val_w = o.shape[1]
    resident = dict(pipeline_mode=pl.Buffered(1))
    return pl.pallas_call(
        _mix_kernel,
        grid=(n // tm,),
        in_specs=[
            pl.BlockSpec((tm, pm.shape[1]), lambda i: (i, 0)),
            pl.BlockSpec((tm, val_w), lambda i: (i, 0)),
            pl.BlockSpec((tm, val_w), lambda i: (i, r_col // val_w)),
            pl.BlockSpec((1, gla_norm.shape[1]), lambda i: (0, 0)),
            pl.BlockSpec(w_a.shape, lambda i: (0, 0), **resident),
            pl.BlockSpec(w_b.shape, lambda i: (0, 0), **resident),
            pl.BlockSpec((tm, d), lambda i: (i, gate_col // d)),
            pl.BlockSpec((tm, d), lambda i: (i, gate_col // d + 1)),
            pl.BlockSpec((2, d), lambda i: (0, 0)),
        ],
        out_specs=pl.BlockSpec((tm, d), lambda i: (i, 0)),
        out_shape=jax.ShapeDtypeStruct((n, d), BF16),
        compiler_params=_params(("parallel",)),
        name="branch_mix",
    )(pm, o, z, gla_norm, w_a, w_b, z, z, bias)


def _outproj_kernel(m_ref, w_ref, x_ref, g_ref, o_ref, *, splits):
    rb = x_ref.shape[0] // splits
    for r0 in range(0, x_ref.shape[0], rb):
        rows = slice(r0, r0 + rb)
        y = _dot(m_ref[rows, :], w_ref[...])
        o_ref[rows, :] = x_ref[rows, :] + _rms(y, g_ref[...])


def _outproj(mixed, w_out, x2, g, tm):
    n, d = x2.shape
    return pl.pallas_call(
        functools.partial(_outproj_kernel, splits=1),
        grid=(n // tm,),
        in_specs=[
            pl.BlockSpec((tm, d), lambda i: (i, 0)),
            pl.BlockSpec((d, d), lambda i: (0, 0)),
            pl.BlockSpec((tm, d), lambda i: (i, 0)),
            pl.BlockSpec((1, d), lambda i: (0, 0)),
        ],
        out_specs=pl.BlockSpec((tm, d), lambda i: (i, 0)),
        out_shape=jax.ShapeDtypeStruct((n, d), F32),
        compiler_params=_params(("parallel",)),
        name="out_proj",
    )(mixed, w_out, x2, g)


def _ffn_kernel(x_ref, gpre_ref, wg_ref, wu_ref, wd_ref, gpost_ref, o_ref, h_ref, acc_ref, *, splits):
    j = pl.program_id(1)
    last = pl.num_programs(1) - 1
    rb = x_ref.shape[0] // splits
    blocks = [slice(r0, r0 + rb) for r0 in range(0, x_ref.shape[0], rb)]

    def hidden_tile(h):
        gate = _dot(h, wg_ref[...])
        up = _dot(h, wu_ref[...])
        act = (gate * jax.nn.sigmoid(gate) * up).astype(BF16)
        return _dot(act, wd_ref[...])

    @pl.when(j == 0)
    def _():
        for rows in blocks:
            h = _rms(x_ref[rows, :], gpre_ref[...]).astype(BF16)
            h_ref[rows, :] = h
            acc_ref[rows, :] = hidden_tile(h)

    @pl.when((j > 0) & (j < last))
    def _():
        acc_ref[...] += hidden_tile(h_ref[...])

    @pl.when(j == last)
    def _():
        for rows in blocks:
            f = acc_ref[rows, :] + hidden_tile(h_ref[rows, :])
            o_ref[rows, :] = x_ref[rows, :] + _rms(f, gpost_ref[...])


def _ffn(x1, g_pre, w_gate, w_up, w_down, g_post, tm, tf):
    n, d = x1.shape
    dff = w_gate.shape[1]
    nf = dff // tf
    ft = lambda i, j: jnp.where(i % 2 == 0, j, nf - 1 - j)
    return pl.pallas_call(
        functools.partial(_ffn_kernel, splits=2),
        grid=(n // tm, nf),
        in_specs=[
            pl.BlockSpec((tm, d), lambda i, j: (i, 0)),
            pl.BlockSpec((1, d), lambda i, j: (0, 0)),
            pl.BlockSpec((d, tf), lambda i, j: (0, ft(i, j))),
            pl.BlockSpec((d, tf), lambda i, j: (0, ft(i, j))),
            pl.BlockSpec((tf, d), lambda i, j: (ft(i, j), 0)),
            pl.BlockSpec((1, d), lambda i, j: (0, 0)),
        ],
        out_specs=pl.BlockSpec((tm, d), lambda i, j: (i, 0)),
        out_shape=jax.ShapeDtypeStruct((n, d), F32),
        scratch_shapes=[pltpu.VMEM((tm, d), BF16), pltpu.VMEM((tm, d), F32)],
        compiler_params=_params(("parallel", "arbitrary")),
        name="ffn",
    )(x1, g_pre, w_gate, w_up, w_down, g_post)


def _layer(x2, batch, seq, norm_mix_pre, w_in, w_gate_up, b_gate, w_pool, pool_scale, gla_norm,
           w_branch_a, w_branch_b, b_branch_gates, w_out, norm_mix_post,
           norm_ffn_pre, w_ffn_gate, w_ffn_up, w_ffn_down, norm_ffn_post):
    d = x2.shape[1]
    pool_w = w_branch_a.shape[0]
    key_w = w_gate_up.shape[1]
    val_w = w_branch_b.shape[0]
    rank = w_gate_up.shape[0]

    o_p, o_q, o_k = 0, pool_w, pool_w + key_w
    o_v = o_k + key_w
    o_g = o_v + val_w
    o_r = o_g + rank
    o_gate = o_r + val_w
    tn = 1024
    w_in_b = w_in.astype(BF16)
    w_hi = w_in_b[:, o_r:]
    n_lo = o_g // tn
    n_front = o_v // tn
    c_v, c_r, c_gate = 0, val_w, 2 * val_w
    c_p = c_gate + 2 * d
    c_q, c_k = c_p + pool_w, c_p + pool_w + key_w
    out_tile = lambda j: jnp.where(j < n_front, j + c_p // tn, j - n_front)
    w_glr = jnp.pad(w_in_b[:, o_g:o_r], ((0, 0), (0, LANES - rank)))
    w_gu = jnp.pad(w_gate_up, ((0, LANES - rank), (0, 0))).astype(BF16)

    z, glr = _inproj(x2, norm_mix_pre[None, :], w_in_b, w_hi, w_glr, n_lo, out_tile, tm=1024, tn=tn)
    pm = _pool(z, w_pool.astype(BF16), pool_scale[None, :], batch, seq, c_p // pool_w, tt=2048)
    o = _gla(z, glr, w_gu, b_gate[None, :], val_w, batch, seq, (c_q, c_k, c_v), tt=512)
    mixed = _mix(pm, o, z, gla_norm[None, :], w_branch_a.astype(BF16), w_branch_b.astype(BF16),
                 b_branch_gates, c_r, c_gate, tm=512)
    x1 = _outproj(mixed, w_out.astype(BF16), x2, norm_mix_post[None, :], tm=512)
    return _ffn(x1, norm_ffn_pre[None, :], w_ffn_gate.astype(BF16), w_ffn_up.astype(BF16),
                w_ffn_down.astype(BF16), norm_ffn_post[None, :], tm=1024, tf=512)


def kernel(x, norm_mix_pre, w_in, w_gate_up, b_gate, w_pool, pool_scale, gla_norm, w_branch_a,
           w_branch_b, b_branch_gates, w_out, norm_mix_post, norm_ffn_pre, w_ffn_gate, w_ffn_up,
           w_ffn_down, norm_ffn_post):
    batch, seq, d = x.shape
    x2 = x.reshape(batch * seq, d)
    params = (norm_mix_pre, w_in, w_gate_up, b_gate, w_pool, pool_scale, gla_norm, w_branch_a,
              w_branch_b, b_branch_gates, w_out, norm_mix_post, norm_ffn_pre, w_ffn_gate,
              w_ffn_up, w_ffn_down, norm_ffn_post)
    for layer in range(norm_mix_pre.shape[0]):
        x2 = _layer(x2, batch, seq, *(p[layer] for p in params))
    return x2.reshape(batch, seq, d)
```

```python
import functools

import jax
import jax.numpy as jnp
from jax import lax
from jax.experimental import pallas as pl
from jax.experimental.pallas import tpu as pltpu

F32 = jnp.float32
BF16 = jnp.bfloat16

EPS = 1e-6
POOL_WINDOWS = (2, 4, 8, 16)
POOL_HALO = 16
GLA_HEADS = 4
GLA_GATE_TAU = 16.0
GLA_CHUNK = 128
LOG2E = 1.4426950408889634
SUBLANES = 8
LANES = 128
VMEM_LIMIT = 62 * 1024 * 1024


def _params(sem):
    return pltpu.CompilerParams(dimension_semantics=sem, vmem_limit_bytes=VMEM_LIMIT)


def _rms(x, g):
    ms = jnp.mean(x * x, axis=-1, keepdims=True)
    return x * lax.rsqrt(ms + EPS) * g


def _dot(a, b):
    return jnp.dot(a, b, preferred_element_type=F32)


def _inproj_kernel(x_ref, g_ref, wlo_ref, whi_ref, wglr_ref, z_ref, glr_ref, h_ref, *, n_lo, splits):
    j = pl.program_id(1)

    @pl.when(j == 0)
    def _():
        rb = x_ref.shape[0] // splits
        for r0 in range(0, x_ref.shape[0], rb):
            rows = slice(r0, r0 + rb)
            hb = _rms(x_ref[rows, :], g_ref[...]).astype(BF16)
            h_ref[rows, :] = hb
            glr_ref[rows, :] = _dot(hb, wglr_ref[...])
            z_ref[rows, :] = _dot(hb, wlo_ref[...]).astype(BF16)

    @pl.when((j > 0) & (j < n_lo))
    def _():
        z_ref[...] = _dot(h_ref[...], wlo_ref[...]).astype(BF16)

    @pl.when(j >= n_lo)
    def _():
        z_ref[...] = _dot(h_ref[...], whi_ref[...]).astype(BF16)


def _inproj(x2, g, w_lo, w_hi, w_glr, n_lo, out_tile, tm, tn):
    n, d = x2.shape
    n_hi = w_hi.shape[1] // tn

    def lo_tile(i, j):
        t = jnp.minimum(j, n_lo - 1)
        return jnp.where(i % 2 == 0, t, n_lo - 1 - t)

    def hi_tile(i, j):
        t = jnp.maximum(j - n_lo, 0)
        return jnp.where(i % 2 == 0, t, n_hi - 1 - t)

    def z_tile(i, j):
        return out_tile(jnp.where(j < n_lo, lo_tile(i, j), n_lo + hi_tile(i, j)))

    return pl.pallas_call(
        functools.partial(_inproj_kernel, n_lo=n_lo, splits=4),
        grid=(n // tm, n_lo + n_hi),
        in_specs=[
            pl.BlockSpec((tm, d), lambda i, j: (i, 0)),
            pl.BlockSpec((1, d), lambda i, j: (0, 0)),
            pl.BlockSpec((d, tn), lambda i, j: (0, lo_tile(i, j))),
            pl.BlockSpec((d, tn), lambda i, j: (0, hi_tile(i, j))),
            pl.BlockSpec((d, LANES), lambda i, j: (0, 0)),
        ],
        out_specs=[
            pl.BlockSpec((tm, tn), lambda i, j: (i, z_tile(i, j))),
            pl.BlockSpec((tm, LANES), lambda i, j: (i, 0)),
        ],
        out_shape=[
            jax.ShapeDtypeStruct((n, (n_lo + n_hi) * tn), BF16),
            jax.ShapeDtypeStruct((n, LANES), F32),
        ],
        scratch_shapes=[pltpu.VMEM((tm, d), BF16)],
        compiler_params=_params(("parallel", "arbitrary")),
        name="in_proj",
    )(x2, g, w_lo, w_hi, w_glr)


def _pool_kernel(p_ref, wp_ref, sc_ref, o_ref, carry_ref, *, tt):
    t = pl.program_id(1)

    @pl.when(t == 0)
    def _():
        carry_ref[...] = jnp.zeros_like(carry_ref)

    p = p_ref[...].astype(F32)
    ext = jnp.concatenate([carry_ref[...], p], axis=0)
    carry_ref[...] = p[tt - POOL_HALO:, :]
    pos = t * tt + lax.broadcasted_iota(jnp.int32, (tt, 1), 0)
    gd = wp_ref.shape[1]
    outs = []
    for g, w in enumerate(POOL_WINDOWS):
        s = ext[:, g * gd:(g + 1) * gd]
        shift = 1
        while shift < w:
            s = s + pltpu.roll(s, shift, 0)
            shift *= 2
        count = jnp.minimum(pos + 1, w).astype(F32)
        d = s[POOL_HALO:, :] / count - p[:, g * gd:(g + 1) * gd]
        y = _dot(d.astype(BF16), wp_ref[g])
        outs.append(y * sc_ref[:, g * gd:(g + 1) * gd])
    o_ref[...] = jnp.concatenate(outs, axis=1).astype(BF16)


def _pool(z, w_pool, scale, batch, seq, col_block, tt):
    width = w_pool.shape[0] * w_pool.shape[1]
    nt = seq // tt
    return pl.pallas_call(
        functools.partial(_pool_kernel, tt=tt),
        grid=(batch, nt),
        in_specs=[
            pl.BlockSpec((tt, width), lambda b, t: (b * nt + t, col_block)),
            pl.BlockSpec(w_pool.shape, lambda b, t: (0, 0, 0)),
            pl.BlockSpec((1, width), lambda b, t: (0, 0)),
        ],
        out_specs=pl.BlockSpec((tt, width), lambda b, t: (b * nt + t, 0)),
        out_shape=jax.ShapeDtypeStruct((batch * seq, width), BF16),
        scratch_shapes=[pltpu.VMEM((POOL_HALO, width), F32)],
        compiler_params=_params(("parallel", "arbitrary")),
        name="pool_mixer",
    )(z, w_pool, scale)


_NT = (((1,), (1,)), ((), ()))
_TN = (((0,), (0,)), ((), ()))


def _store_tiles(ref, x):
    for j in range(ref.shape[0]):
        ref[j] = x[:, j * LANES:(j + 1) * LANES]


def _rows(ref, r0, n):
    return jnp.concatenate([ref[j, pl.ds(r0, n), :] for j in range(ref.shape[0])], axis=1)


def _bcast_row(ref, r, n):
    return jnp.concatenate([ref[j, pl.ds(r, n, stride=0), :] for j in range(ref.shape[0])], axis=1)


def _gla_kernel(q_ref, k_ref, v_ref, glr_ref, wgu_ref, bg_ref,
                o_ref, s_ref, g_scr, q_scr, k_scr, *, tt, scale):
    c = GLA_CHUNK
    kw = q_ref.shape[1]
    dk = kw // GLA_HEADS
    dv = v_ref.shape[1] // GLA_HEADS
    heads = range(GLA_HEADS)

    @pl.when(pl.program_id(1) == 0)
    def _():
        s_ref[...] = jnp.zeros_like(s_ref)

    row = lax.broadcasted_iota(jnp.int32, (c, c), 0)
    col = lax.broadcasted_iota(jnp.int32, (c, c), 1)
    tri = (row >= col).astype(BF16)
    sel_r = lax.broadcasted_iota(jnp.int32, (LANES, c), 0)
    sel_c = lax.broadcasted_iota(jnp.int32, (LANES, c), 1)
    sel = (sel_r == (sel_c % SUBLANES)).astype(BF16)
    lane = lax.broadcasted_iota(jnp.int32, (SUBLANES, LANES), 1)
    sub = lax.broadcasted_iota(jnp.int32, (SUBLANES, LANES), 0)
    slot_mask = [(lane == s) & (sub >= s) for s in range(SUBLANES)]
    in_group = ((row // SUBLANES) == (col // SUBLANES)) & (row >= col)
    level_sizes = []
    b = c // 2
    while b >= 2 * SUBLANES:
        level_sizes.append(b)
        b //= 2
    same_block = [(row // b) == (col // b) for b in level_sizes]

    for ci in range(tt // c):
        rows = slice(ci * c, (ci + 1) * c)
        u = _dot(glr_ref[rows, :].astype(BF16), wgu_ref[...]) + bg_ref[...]
        la = (jnp.minimum(u, 0.0) - jnp.log(1.0 + jnp.exp(-jnp.abs(u)))) * (LOG2E / GLA_GATE_TAU)
        la_hi = la.astype(BF16)
        la_lo = (la - la_hi.astype(F32)).astype(BF16)
        gcum = _dot(tri, la_hi) + _dot(tri, la_lo)
        _store_tiles(g_scr, gcum)
        _store_tiles(q_scr, q_ref[rows, :].astype(F32) * scale)
        _store_tiles(k_scr, k_ref[rows, :].astype(F32))

        def level(b):
            half = b // 2
            qs, ks = [], []
            for s in range(0, c, b):
                gref = _bcast_row(g_scr, s + half, half)
                ks.append((_rows(k_scr, s, half) * jnp.exp2(gref - _rows(g_scr, s, half))).astype(BF16))
                ks.append(jnp.zeros((half, kw), BF16))
                qs.append(jnp.zeros((half, kw), BF16))
                qs.append((_rows(q_scr, s + half, half)
                           * jnp.exp2(_rows(g_scr, s + half, half) - gref)).astype(BF16))
            qb, kb = jnp.concatenate(qs, 0), jnp.concatenate(ks, 0)
            return [lax.dot_general(qb[:, h * dk:(h + 1) * dk], kb[:, h * dk:(h + 1) * dk], _NT,
                                    preferred_element_type=F32) for h in heads]

        a = level(c)
        for b, mask in zip(level_sizes, same_block):
            a = [jnp.where(mask, new, old) for new, old in zip(level(b), a)]

        groups = [[] for _ in heads]
        for g0 in range(0, c, SUBLANES):
            qg, gg = _rows(q_scr, g0, SUBLANES), _rows(g_scr, g0, SUBLANES)
            slots = [jnp.zeros((SUBLANES, LANES), F32) for _ in heads]
            for s in range(SUBLANES):
                p = qg * _bcast_row(k_scr, g0 + s, SUBLANES) * jnp.exp2(gg - _bcast_row(g_scr, g0 + s, SUBLANES))
                for h in heads:
                    score = jnp.sum(p[:, h * dk:(h + 1) * dk], axis=1, keepdims=True)
                    slots[h] = jnp.where(slot_mask[s], score, slots[h])
            for h in heads:
                groups[h].append(slots[h])
        a = [jnp.where(in_group, _dot(jnp.concatenate(groups[h], 0).astype(BF16), sel), a[h]).astype(BF16)
             for h in heads]

        gcum = _rows(g_scr, 0, c)
        g_last8 = _bcast_row(g_scr, c - 1, SUBLANES)
        g_last = jnp.concatenate([g_last8] * (c // SUBLANES), 0)
        qg = (_rows(q_scr, 0, c) * jnp.exp2(gcum)).astype(BF16)
        kd = (_rows(k_scr, 0, c) * jnp.exp2(g_last - gcum)).astype(BF16)
        decay_t = jnp.concatenate([jnp.exp2(g_last8)] * (LANES // SUBLANES), 0)
        for h in heads:
            kc = slice(h * dk, (h + 1) * dk)
            vc = slice(h * dv, (h + 1) * dv)
            v = v_ref[rows, vc]
            s_old = s_ref[h]
            o = _dot(qg[:, kc], s_old.astype(BF16)) + _dot(a[h], v)
            decay = decay_t[:, kc].T
            decay = jnp.concatenate([decay] * (dv // LANES), axis=1)
            s_ref[h] = decay * s_old + lax.dot_general(kd[:, kc], v, _TN, preferred_element_type=F32)
            o_ref[rows, vc] = o.astype(BF16)


def _gla(z, glr, w_gate_up, b_gate, val_w, batch, seq, cols, tt):
    key_w = w_gate_up.shape[1]
    dk = key_w // GLA_HEADS
    dv = val_w // GLA_HEADS
    nt = seq // tt
    q0, k0, v0 = cols
    tiles = key_w // LANES
    return pl.pallas_call(
        functools.partial(_gla_kernel, tt=tt, scale=dk ** -0.5),
        grid=(batch, nt),
        in_specs=[
            pl.BlockSpec((tt, key_w), lambda b, t: (b * nt + t, q0 // key_w)),
            pl.BlockSpec((tt, key_w), lambda b, t: (b * nt + t, k0 // key_w)),
            pl.BlockSpec((tt, val_w), lambda b, t: (b * nt + t, v0 // val_w)),
            pl.BlockSpec((tt, LANES), lambda b, t: (b * nt + t, 0)),
            pl.BlockSpec((LANES, key_w), lambda b, t: (0, 0)),
            pl.BlockSpec((1, key_w), lambda b, t: (0, 0)),
        ],
        out_specs=pl.BlockSpec((tt, val_w), lambda b, t: (b * nt + t, 0)),
        out_shape=jax.ShapeDtypeStruct((batch * seq, val_w), BF16),
        scratch_shapes=[
            pltpu.VMEM((GLA_HEADS, dk, dv), F32),
            pltpu.VMEM((tiles, GLA_CHUNK, LANES), F32),
            pltpu.VMEM((tiles, GLA_CHUNK, LANES), F32),
            pltpu.VMEM((tiles, GLA_CHUNK, LANES), F32),
        ],
        compiler_params=_params(("parallel", "arbitrary")),
        name="gla",
    )(z, z, z, glr, w_gate_up, b_gate)


def _mix_kernel(pm_ref, o_ref, r_ref, gn_ref, wa_ref, wb_ref, ga_ref, gb_ref, bias_ref,
                out_ref):
    dv = gn_ref.shape[1]
    y_b = None
    for h in range(o_ref.shape[1] // dv):
        vc = slice(h * dv, (h + 1) * dv)
        r = r_ref[:, vc].astype(F32)
        on = (_rms(o_ref[:, vc].astype(F32), gn_ref[...]) * (r * jax.nn.sigmoid(r))).astype(BF16)
        part = _dot(on, wb_ref[vc, :])
        y_b = part if y_b is None else y_b + part
    y_a = _dot(pm_ref[...], wa_ref[...])
    gate_a = jax.nn.sigmoid(ga_ref[...].astype(F32) + bias_ref[0:1, :])
    gate_b = jax.nn.sigmoid(gb_ref[...].astype(F32) + bias_ref[1:2, :])
    out_ref[...] = (gate_a * y_a + gate_b * y_b).astype(BF16)


def _mix(pm, o, z, gla_norm, w_a, w_b, bias, r_col, gate_col, tm):
    n = pm.shape[0]
    d = w_a.shape[1]
    val_w = o.shape[1]
    resident = dict(pipeline_mode=pl.Buffered(1))
    return pl.pallas_call(
        _mix_kernel,
        grid=(n // tm,),
        in_specs=[
            pl.BlockSpec((tm, pm.shape[1]), lambda i: (i, 0)),
            pl.BlockSpec((tm, val_w), lambda i: (i, 0)),
            pl.BlockSpec((tm, val_w), lambda i: (i, r_col // val_w)),
            pl.BlockSpec((1, gla_norm.shape[1]), lambda i: (0, 0)),
            pl.BlockSpec(w_a.shape, lambda i: (0, 0), **resident),
            pl.BlockSpec(w_b.shape, lambda i: (0, 0), **resident),
            pl.BlockSpec((tm, d), lambda i: (i, gate_col // d)),
            pl.BlockSpec((tm, d), lambda i: (i, gate_col // d + 1)),
            pl.BlockSpec((2, d), lambda i: (0, 0)),
        ],
        out_specs=pl.BlockSpec((tm, d), lambda i: (i, 0)),
        out_shape=jax.ShapeDtypeStruct((n, d), BF16),
        compiler_params=_params(("parallel",)),
        name="branch_mix",
    )(pm, o, z, gla_norm, w_a, w_b, z, z, bias)


def _outproj_kernel(m_ref, w_ref, x_ref, g_ref, o_ref, *, splits):
    rb = x_ref.shape[0] // splits
    for r0 in range(0, x_ref.shape[0], rb):
        rows = slice(r0, r0 + rb)
        y = _dot(m_ref[rows, :], w_ref[...])
        o_ref[rows, :] = x_ref[rows, :] + _rms(y, g_ref[...])


def _outproj(mixed, w_out, x2, g, tm):
    n, d = x2.shape
    return pl.pallas_call(
        functools.partial(_outproj_kernel, splits=1),
        grid=(n // tm,),
        in_specs=[
            pl.BlockSpec((tm, d), lambda i: (i, 0)),
            pl.BlockSpec((d, d), lambda i: (0, 0)),
            pl.BlockSpec((tm, d), lambda i: (i, 0)),
            pl.BlockSpec((1, d), lambda i: (0, 0)),
        ],
        out_specs=pl.BlockSpec((tm, d), lambda i: (i, 0)),
        out_shape=jax.ShapeDtypeStruct((n, d), F32),
        compiler_params=_params(("parallel",)),
        name="out_proj",
    )(mixed, w_out, x2, g)


def _ffn_kernel(x_ref, gpre_ref, wg_ref, wu_ref, wd_ref, gpost_ref, o_ref, h_ref, acc_ref, *, splits):
    j = pl.program_id(1)
    last = pl.num_programs(1) - 1
    rb = x_ref.shape[0] // splits
    blocks = [slice(r0, r0 + rb) for r0 in range(0, x_ref.shape[0], rb)]

    def hidden_tile(h):
        gate = _dot(h, wg_ref[...])
        up = _dot(h, wu_ref[...])
        act = (gate * jax.nn.sigmoid(gate) * up).astype(BF16)
        return _dot(act, wd_ref[...])

    @pl.when(j == 0)
    def _():
        for rows in blocks:
            h = _rms(x_ref[rows, :], gpre_ref[...]).astype(BF16)
            h_ref[rows, :] = h
            acc_ref[rows, :] = hidden_tile(h)

    @pl.when((j > 0) & (j < last))
    def _():
        acc_ref[...] += hidden_tile(h_ref[...])

    @pl.when(j == last)
    def _():
        for rows in blocks:
            f = acc_ref[rows, :] + hidden_tile(h_ref[rows, :])
            o_ref[rows, :] = x_ref[rows, :] + _rms(f, gpost_ref[...])


def _ffn(x1, g_pre, w_gate, w_up, w_down, g_post, tm, tf):
    n, d = x1.shape
    dff = w_gate.shape[1]
    nf = dff // tf
    ft = lambda i, j: jnp.where(i % 2 == 0, j, nf - 1 - j)
    return pl.pallas_call(
        functools.partial(_ffn_kernel, splits=2),
        grid=(n // tm, nf),
        in_specs=[
            pl.BlockSpec((tm, d), lambda i, j: (i, 0)),
            pl.BlockSpec((1, d), lambda i, j: (0, 0)),
            pl.BlockSpec((d, tf), lambda i, j: (0, ft(i, j))),
            pl.BlockSpec((d, tf), lambda i, j: (0, ft(i, j))),
            pl.BlockSpec((tf, d), lambda i, j: (ft(i, j), 0)),
            pl.BlockSpec((1, d), lambda i, j: (0, 0)),
        ],
        out_specs=pl.BlockSpec((tm, d), lambda i, j: (i, 0)),
        out_shape=jax.ShapeDtypeStruct((n, d), F32),
        scratch_shapes=[pltpu.VMEM((tm, d), BF16), pltpu.VMEM((tm, d), F32)],
        compiler_params=_params(("parallel", "arbitrary")),
        name="ffn",
    )(x1, g_pre, w_gate, w_up, w_down, g_post)


def _layer(x2, batch, seq, norm_mix_pre, w_in, w_gate_up, b_gate, w_pool, pool_scale, gla_norm,
           w_branch_a, w_branch_b, b_branch_gates, w_out, norm_mix_post,
           norm_ffn_pre, w_ffn_gate, w_ffn_up, w_ffn_down, norm_ffn_post):
    d = x2.shape[1]
    pool_w = w_branch_a.shape[0]
    key_w = w_gate_up.shape[1]
    val_w = w_branch_b.shape[0]
    rank = w_gate_up.shape[0]

    o_p, o_q, o_k = 0, pool_w, pool_w + key_w
    o_v = o_k + key_w
    o_g = o_v + val_w
    o_r = o_g + rank
    o_gate = o_r + val_w
    tn = 1024
    w_lo = w_in[:, :o_g].astype(BF16)
    w_hi = w_in[:, o_r:].astype(BF16)
    n_lo = o_g // tn
    n_front = o_v // tn
    c_v, c_r, c_gate = 0, val_w, 2 * val_w
    c_p = c_gate + 2 * d
    c_q, c_k = c_p + pool_w, c_p + pool_w + key_w
    out_tile = lambda j: jnp.where(j < n_front, j + c_p // tn, j - n_front)
    w_glr = jnp.pad(w_in[:, o_g:o_r], ((0, 0), (0, LANES - rank))).astype(BF16)
    w_gu = jnp.pad(w_gate_up, ((0, LANES - rank), (0, 0))).astype(BF16)

    z, glr = _inproj(x2, norm_mix_pre[None, :], w_lo, w_hi, w_glr, n_lo, out_tile, tm=1024, tn=tn)
    pm = _pool(z, w_pool.astype(BF16), pool_scale[None, :], batch, seq, c_p // pool_w, tt=2048)
    o = _gla(z, glr, w_gu, b_gate[None, :], val_w, batch, seq, (c_q, c_k, c_v), tt=512)
    mixed = _mix(pm, o, z, gla_norm[None, :], w_branch_a.astype(BF16), w_branch_b.astype(BF16),
                 b_branch_gates, c_r, c_gate, tm=512)
    x1 = _outproj(mixed, w_out.astype(BF16), x2, norm_mix_post[None, :], tm=512)
    return _ffn(x1, norm_ffn_pre[None, :], w_ffn_gate.astype(BF16), w_ffn_up.astype(BF16),
                w_ffn_down.astype(BF16), norm_ffn_post[None, :], tm=1024, tf=512)


def kernel(x, norm_mix_pre, w_in, w_gate_up, b_gate, w_pool, pool_scale, gla_norm, w_branch_a,
           w_branch_b, b_branch_gates, w_out, norm_mix_post, norm_ffn_pre, w_ffn_gate, w_ffn_up,
           w_ffn_down, norm_ffn_post):
    batch, seq, d = x.shape
    x2 = x.reshape(batch * seq, d)
    params = (norm_mix_pre, w_in, w_gate_up, b_gate, w_pool, pool_scale, gla_norm, w_branch_a,
              w_branch_b, b_branch_gates, w_out, norm_mix_post, norm_ffn_pre, w_ffn_gate,
              w_ffn_up, w_ffn_down, norm_ffn_post)
    for layer in range(norm_mix_pre.shape[0]):
        x2 = _layer(x2, batch, seq, *(p[layer] for p in params))
    return x2.reshape(batch, seq, d)
```

```python
import functools

import jax
import jax.numpy as jnp
from jax import lax
from jax.experimental import pallas as pl
from jax.experimental.pallas import tpu as pltpu

F32 = jnp.float32
BF16 = jnp.bfloat16

EPS = 1e-6
POOL_WINDOWS = (2, 4, 8, 16)
POOL_HALO = 16
GLA_HEADS = 4
GLA_GATE_TAU = 16.0
GLA_CHUNK = 128
LOG2E = 1.4426950408889634
SUBLANES = 8
LANES = 128
VMEM_LIMIT = 62 * 1024 * 1024


def _params(sem):
    return pltpu.CompilerParams(dimension_semantics=sem, vmem_limit_bytes=VMEM_LIMIT)


def _rms(x, g):
    ms = jnp.mean(x * x, axis=-1, keepdims=True)
    return x * lax.rsqrt(ms + EPS) * g


def _dot(a, b):
    return jnp.dot(a, b, preferred_element_type=F32)


def _inproj_kernel(x_ref, g_ref, wlo_ref, whi_ref, wglr_ref, z_ref, glr_ref, h_ref, *, n_lo, splits):
    j = pl.program_id(1)

    @pl.when(j == 0)
    def _():
        rb = x_ref.shape[0] // splits
        for r0 in range(0, x_ref.shape[0], rb):
            rows = slice(r0, r0 + rb)
            hb = _rms(x_ref[rows, :], g_ref[...]).astype(BF16)
            h_ref[rows, :] = hb
            glr_ref[rows, :] = _dot(hb, wglr_ref[...])
            z_ref[rows, :] = _dot(hb, wlo_ref[...]).astype(BF16)

    @pl.when((j > 0) & (j < n_lo))
    def _():
        z_ref[...] = _dot(h_ref[...], wlo_ref[...]).astype(BF16)

    @pl.when(j >= n_lo)
    def _():
        z_ref[...] = _dot(h_ref[...], whi_ref[...]).astype(BF16)


def _inproj(x2, g, w_lo, w_hi, w_glr, n_lo, out_tile, tm, tn):
    n, d = x2.shape
    n_hi = w_hi.shape[1] // tn

    def lo_tile(i, j):
        t = jnp.minimum(j, n_lo - 1)
        return jnp.where(i % 2 == 0, t, n_lo - 1 - t)

    def hi_tile(i, j):
        t = jnp.maximum(j - n_lo, 0)
        return jnp.where(i % 2 == 0, t, n_hi - 1 - t)

    def z_tile(i, j):
        return out_tile(jnp.where(j < n_lo, lo_tile(i, j), n_lo + hi_tile(i, j)))

    return pl.pallas_call(
        functools.partial(_inproj_kernel, n_lo=n_lo, splits=4),
        grid=(n // tm, n_lo + n_hi),
        in_specs=[
            pl.BlockSpec((tm, d), lambda i, j: (i, 0)),
            pl.BlockSpec((1, d), lambda i, j: (0, 0)),
            pl.BlockSpec((d, tn), lambda i, j: (0, lo_tile(i, j))),
            pl.BlockSpec((d, tn), lambda i, j: (0, hi_tile(i, j))),
            pl.BlockSpec((d, LANES), lambda i, j: (0, 0)),
        ],
        out_specs=[
            pl.BlockSpec((tm, tn), lambda i, j: (i, z_tile(i, j))),
            pl.BlockSpec((tm, LANES), lambda i, j: (i, 0)),
        ],
        out_shape=[
            jax.ShapeDtypeStruct((n, (n_lo + n_hi) * tn), BF16),
            jax.ShapeDtypeStruct((n, LANES), F32),
        ],
        scratch_shapes=[pltpu.VMEM((tm, d), BF16)],
        compiler_params=_params(("parallel", "arbitrary")),
        name="in_proj",
    )(x2, g, w_lo, w_hi, w_glr)


def _pool_kernel(p_ref, wp_ref, sc_ref, o_ref, carry_ref, *, tt):
    t = pl.program_id(1)

    @pl.when(t == 0)
    def _():
        carry_ref[...] = jnp.zeros_like(carry_ref)

    p = p_ref[...].astype(F32)
    ext = jnp.concatenate([carry_ref[...], p], axis=0)
    carry_ref[...] = p[tt - POOL_HALO:, :]
    pos = t * tt + lax.broadcasted_iota(jnp.int32, (tt, 1), 0)
    gd = wp_ref.shape[1]
    outs = []
    for g, w in enumerate(POOL_WINDOWS):
        s = ext[:, g * gd:(g + 1) * gd]
        shift = 1
        while shift < w:
            s = s + pltpu.roll(s, shift, 0)
            shift *= 2
        count = jnp.minimum(pos + 1, w).astype(F32)
        d = s[POOL_HALO:, :] / count - p[:, g * gd:(g + 1) * gd]
        y = _dot(d.astype(BF16), wp_ref[g])
        outs.append(y * sc_ref[:, g * gd:(g + 1) * gd])
    o_ref[...] = jnp.concatenate(outs, axis=1).astype(BF16)


def _pool(z, w_pool, scale, batch, seq, col_block, tt):
    width = w_pool.shape[0] * w_pool.shape[1]
    nt = seq // tt
    return pl.pallas_call(
        functools.partial(_pool_kernel, tt=tt),
        grid=(batch, nt),
        in_specs=[
            pl.BlockSpec((tt, width), lambda b, t: (b * nt + t, col_block)),
            pl.BlockSpec(w_pool.shape, lambda b, t: (0, 0, 0)),
            pl.BlockSpec((1, width), lambda b, t: (0, 0)),
        ],
        out_specs=pl.BlockSpec((tt, width), lambda b, t: (b * nt + t, 0)),
        out_shape=jax.ShapeDtypeStruct((batch * seq, width), BF16),
        scratch_shapes=[pltpu.VMEM((POOL_HALO, width), F32)],
        compiler_params=_params(("parallel", "arbitrary")),
        name="pool_mixer",
    )(z, w_pool, scale)


_NT = (((1,), (1,)), ((), ()))
_TN = (((0,), (0,)), ((), ()))


def _store_tiles(ref, x):
    for j in range(ref.shape[0]):
        ref[j] = x[:, j * LANES:(j + 1) * LANES]


def _rows(ref, r0, n):
    return jnp.concatenate([ref[j, pl.ds(r0, n), :] for j in range(ref.shape[0])], axis=1)


def _bcast_row(ref, r, n):
    return jnp.concatenate([ref[j, pl.ds(r, n, stride=0), :] for j in range(ref.shape[0])], axis=1)


def _gla_kernel(q_ref, k_ref, v_ref, glr_ref, wgu_ref, bg_ref,
                o_ref, s_ref, g_scr, q_scr, k_scr, *, tt, scale):
    c = GLA_CHUNK
    kw = q_ref.shape[1]
    dk = kw // GLA_HEADS
    dv = v_ref.shape[1] // GLA_HEADS
    heads = range(GLA_HEADS)

    @pl.when(pl.program_id(1) == 0)
    def _():
        s_ref[...] = jnp.zeros_like(s_ref)

    row = lax.broadcasted_iota(jnp.int32, (c, c), 0)
    col = lax.broadcasted_iota(jnp.int32, (c, c), 1)
    tri = (row >= col).astype(BF16)
    sel_r = lax.broadcasted_iota(jnp.int32, (LANES, c), 0)
    sel_c = lax.broadcasted_iota(jnp.int32, (LANES, c), 1)
    sel = (sel_r == (sel_c % SUBLANES)).astype(BF16)
    lane = lax.broadcasted_iota(jnp.int32, (SUBLANES, LANES), 1)
    sub = lax.broadcasted_iota(jnp.int32, (SUBLANES, LANES), 0)
    slot_mask = [(lane == s) & (sub >= s) for s in range(SUBLANES)]
    in_group = ((row // SUBLANES) == (col // SUBLANES)) & (row >= col)
    level_sizes = []
    b = c // 2
    while b >= 2 * SUBLANES:
        level_sizes.append(b)
        b //= 2
    same_block = [(row // b) == (col // b) for b in level_sizes]

    for ci in range(tt // c):
        rows = slice(ci * c, (ci + 1) * c)
        u = _dot(glr_ref[rows, :].astype(BF16), wgu_ref[...]) + bg_ref[...]
        la = (jnp.minimum(u, 0.0) - jnp.log(1.0 + jnp.exp(-jnp.abs(u)))) * (LOG2E / GLA_GATE_TAU)
        la_hi = la.astype(BF16)
        la_lo = (la - la_hi.astype(F32)).astype(BF16)
        gcum = _dot(tri, la_hi) + _dot(tri, la_lo)
        _store_tiles(g_scr, gcum)
        _store_tiles(q_scr, q_ref[rows, :].astype(F32) * scale)
        _store_tiles(k_scr, k_ref[rows, :].astype(F32))

        def level(b):
            half = b // 2
            qs, ks = [], []
            for s in range(0, c, b):
                gref = _bcast_row(g_scr, s + half, half)
                ks.append((_rows(k_scr, s, half) * jnp.exp2(gref - _rows(g_scr, s, half))).astype(BF16))
                ks.append(jnp.zeros((half, kw), BF16))
                qs.append(jnp.zeros((half, kw), BF16))
                qs.append((_rows(q_scr, s + half, half)
                           * jnp.exp2(_rows(g_scr, s + half, half) - gref)).astype(BF16))
            qb, kb = jnp.concatenate(qs, 0), jnp.concatenate(ks, 0)
            return [lax.dot_general(qb[:, h * dk:(h + 1) * dk], kb[:, h * dk:(h + 1) * dk], _NT,
                                    preferred_element_type=F32) for h in heads]

        a = level(c)
        for b, mask in zip(level_sizes, same_block):
            a = [jnp.where(mask, new, old) for new, old in zip(level(b), a)]

        groups = [[] for _ in heads]
        for g0 in range(0, c, SUBLANES):
            qg, gg = _rows(q_scr, g0, SUBLANES), _rows(g_scr, g0, SUBLANES)
            slots = [jnp.zeros((SUBLANES, LANES), F32) for _ in heads]
            for s in range(SUBLANES):
                p = qg * _bcast_row(k_scr, g0 + s, SUBLANES) * jnp.exp2(gg - _bcast_row(g_scr, g0 + s, SUBLANES))
                for h in heads:
                    score = jnp.sum(p[:, h * dk:(h + 1) * dk], axis=1, keepdims=True)
                    slots[h] = jnp.where(slot_mask[s], score, slots[h])
            for h in heads:
                groups[h].append(slots[h])
        a = [jnp.where(in_group, _dot(jnp.concatenate(groups[h], 0).astype(BF16), sel), a[h]).astype(BF16)
             for h in heads]

        gcum = _rows(g_scr, 0, c)
        g_last8 = _bcast_row(g_scr, c - 1, SUBLANES)
        g_last = jnp.concatenate([g_last8] * (c // SUBLANES), 0)
        qg = (_rows(q_scr, 0, c) * jnp.exp2(gcum)).astype(BF16)
        kd = (_rows(k_scr, 0, c) * jnp.exp2(g_last - gcum)).astype(BF16)
        decay_t = jnp.concatenate([jnp.exp2(g_last8)] * (LANES // SUBLANES), 0)
        for h in heads:
            kc = slice(h * dk, (h + 1) * dk)
            vc = slice(h * dv, (h + 1) * dv)
            v = v_ref[rows, vc]
            s_old = s_ref[h]
            o = _dot(qg[:, kc], s_old.astype(BF16)) + _dot(a[h], v)
            decay = decay_t[:, kc].T
            decay = jnp.concatenate([decay] * (dv // LANES), axis=1)
            s_ref[h] = decay * s_old + lax.dot_general(kd[:, kc], v, _TN, preferred_element_type=F32)
            o_ref[rows, vc] = o.astype(BF16)


def _gla(z, glr, w_gate_up, b_gate, val_w, batch, seq, cols, tt):
    key_w = w_gate_up.shape[1]
    dk = key_w // GLA_HEADS
    dv = val_w // GLA_HEADS
    nt = seq // tt
    q0, k0, v0 = cols
    tiles = key_w // LANES
    return pl.pallas_call(
        functools.partial(_gla_kernel, tt=tt, scale=dk ** -0.5),
        grid=(batch, nt),
        in_specs=[
            pl.BlockSpec((tt, key_w), lambda b, t: (b * nt + t, q0 // key_w)),
            pl.BlockSpec((tt, key_w), lambda b, t: (b * nt + t, k0 // key_w)),
            pl.BlockSpec((tt, val_w), lambda b, t: (b * nt + t, v0 // val_w)),
            pl.BlockSpec((tt, LANES), lambda b, t: (b * nt + t, 0)),
            pl.BlockSpec((LANES, key_w), lambda b, t: (0, 0)),
            pl.BlockSpec((1, key_w), lambda b, t: (0, 0)),
        ],
        out_specs=pl.BlockSpec((tt, val_w), lambda b, t: (b * nt + t, 0)),
        out_shape=jax.ShapeDtypeStruct((batch * seq, val_w), BF16),
        scratch_shapes=[
            pltpu.VMEM((GLA_HEADS, dk, dv), F32),
            pltpu.VMEM((tiles, GLA_CHUNK, LANES), F32),
            pltpu.VMEM((tiles, GLA_CHUNK, LANES), F32),
            pltpu.VMEM((tiles, GLA_CHUNK, LANES), F32),
        ],
        compiler_params=_params(("parallel", "arbitrary")),
        name="gla",
    )(z, z, z, glr, w_gate_up, b_gate)


def _mix_kernel(pm_ref, o_ref, r_ref, gn_ref, wa_ref, wb_ref, ga_ref, gb_ref, bias_ref,
                out_ref):
    dv = gn_ref.shape[1]
    y_b = None
    for h in range(o_ref.shape[1] // dv):
        vc = slice(h * dv, (h + 1) * dv)
        r = r_ref[:, vc].astype(F32)
        on = (_rms(o_ref[:, vc].astype(F32), gn_ref[...]) * (r * jax.nn.sigmoid(r))).astype(BF16)
        part = _dot(on, wb_ref[vc, :])
        y_b = part if y_b is None else y_b + part
    y_a = _dot(pm_ref[...], wa_ref[...])
    gate_a = jax.nn.sigmoid(ga_ref[...].astype(F32) + bias_ref[0:1, :])
    gate_b = jax.nn.sigmoid(gb_ref[...].astype(F32) + bias_ref[1:2, :])
    out_ref[...] = (gate_a * y_a + gate_b * y_b).astype(BF16)


def _mix(pm, o, z, gla_norm, w_a, w_b, bias, r_col, gate_col, tm):
    n = pm.shape[0]
    d = w_a.shape[1]
    val_w = o.shape[1]
    resident = dict(pipeline_mode=pl.Buffered(1))
    return pl.pallas_call(
        _mix_kernel,
        grid=(n // tm,),
        in_specs=[
            pl.BlockSpec((tm, pm.shape[1]), lambda i: (i, 0)),
            pl.BlockSpec((tm, val_w), lambda i: (i, 0)),
            pl.BlockSpec((tm, val_w), lambda i: (i, r_col // val_w)),
            pl.BlockSpec((1, gla_norm.shape[1]), lambda i: (0, 0)),
            pl.BlockSpec(w_a.shape, lambda i: (0, 0), **resident),
            pl.BlockSpec(w_b.shape, lambda i: (0, 0), **resident),
            pl.BlockSpec((tm, d), lambda i: (i, gate_col // d)),
            pl.BlockSpec((tm, d), lambda i: (i, gate_col // d + 1)),
            pl.BlockSpec((2, d), lambda i: (0, 0)),
        ],
        out_specs=pl.BlockSpec((tm, d), lambda i: (i, 0)),
        out_shape=jax.ShapeDtypeStruct((n, d), BF16),
        compiler_params=_params(("parallel",)),
        name="branch_mix",
    )(pm, o, z, gla_norm, w_a, w_b, z, z, bias)


def _outproj_kernel(m_ref, w_ref, x_ref, g_ref, o_ref):
    y = _dot(m_ref[...], w_ref[...])
    o_ref[...] = x_ref[...] + _rms(y, g_ref[...])


def _outproj(mixed, w_out, x2, g, tm):
    n, d = x2.shape
    return pl.pallas_call(
        _outproj_kernel,
        grid=(n // tm,),
        in_specs=[
            pl.BlockSpec((tm, d), lambda i: (i, 0)),
            pl.BlockSpec((d, d), lambda i: (0, 0), pipeline_mode=pl.Buffered(1)),
            pl.BlockSpec((tm, d), lambda i: (i, 0)),
            pl.BlockSpec((1, d), lambda i: (0, 0)),
        ],
        out_specs=pl.BlockSpec((tm, d), lambda i: (i, 0)),
        out_shape=jax.ShapeDtypeStruct((n, d), F32),
        compiler_params=_params(("parallel",)),
        name="out_proj",
    )(mixed, w_out, x2, g)


def _ffn_kernel(x_ref, gpre_ref, wg_ref, wu_ref, wd_ref, gpost_ref, o_ref, h_ref, acc_ref, *, splits):
    j = pl.program_id(1)
    last = pl.num_programs(1) - 1
    rb = x_ref.shape[0] // splits
    blocks = [slice(r0, r0 + rb) for r0 in range(0, x_ref.shape[0], rb)]

    def hidden_tile(h):
        gate = _dot(h, wg_ref[...])
        up = _dot(h, wu_ref[...])
        act = (gate * jax.nn.sigmoid(gate) * up).astype(BF16)
        return _dot(act, wd_ref[...])

    @pl.when(j == 0)
    def _():
        for rows in blocks:
            h = _rms(x_ref[rows, :], gpre_ref[...]).astype(BF16)
            h_ref[rows, :] = h
            acc_ref[rows, :] = hidden_tile(h)

    @pl.when((j > 0) & (j < last))
    def _():
        acc_ref[...] += hidden_tile(h_ref[...])

    @pl.when(j == last)
    def _():
        for rows in blocks:
            f = acc_ref[rows, :] + hidden_tile(h_ref[rows, :])
            o_ref[rows, :] = x_ref[rows, :] + _rms(f, gpost_ref[...])


def _ffn(x1, g_pre, w_gate, w_up, w_down, g_post, tm, tf):
    n, d = x1.shape
    dff = w_gate.shape[1]
    nf = dff // tf
    ft = lambda i, j: jnp.where(i % 2 == 0, j, nf - 1 - j)
    return pl.pallas_call(
        functools.partial(_ffn_kernel, splits=2),
        grid=(n // tm, nf),
        in_specs=[
            pl.BlockSpec((tm, d), lambda i, j: (i, 0)),
            pl.BlockSpec((1, d), lambda i, j: (0, 0)),
            pl.BlockSpec((d, tf), lambda i, j: (0, ft(i, j))),
            pl.BlockSpec((d, tf), lambda i, j: (0, ft(i, j))),
            pl.BlockSpec((tf, d), lambda i, j: (ft(i, j), 0)),
            pl.BlockSpec((1, d), lambda i, j: (0, 0)),
        ],
        out_specs=pl.BlockSpec((tm, d), lambda i, j: (i, 0)),
        out_shape=jax.ShapeDtypeStruct((n, d), F32),
        scratch_shapes=[pltpu.VMEM((tm, d), BF16), pltpu.VMEM((tm, d), F32)],
        compiler_params=_params(("parallel", "arbitrary")),
        name="ffn",
    )(x1, g_pre, w_gate, w_up, w_down, g_post)


def _layer(x2, batch, seq, norm_mix_pre, w_in, w_gate_up, b_gate, w_pool, pool_scale, gla_norm,
           w_branch_a, w_branch_b, b_branch_gates, w_out, norm_mix_post,
           norm_ffn_pre, w_ffn_gate, w_ffn_up, w_ffn_down, norm_ffn_post):
    d = x2.shape[1]
    pool_w = w_branch_a.shape[0]
    key_w = w_gate_up.shape[1]
    val_w = w_branch_b.shape[0]
    rank = w_gate_up.shape[0]

    o_p, o_q, o_k = 0, pool_w, pool_w + key_w
    o_v = o_k + key_w
    o_g = o_v + val_w
    o_r = o_g + rank
    o_gate = o_r + val_w
    tn = 1024
    w_in_b = w_in.astype(BF16)
    w_hi = w_in_b[:, o_r:]
    n_lo = o_g // tn
    n_front = o_v // tn
    c_v, c_r, c_gate = 0, val_w, 2 * val_w
    c_p = c_gate + 2 * d
    c_q, c_k = c_p + pool_w, c_p + pool_w + key_w
    out_tile = lambda j: jnp.where(j < n_front, j + c_p // tn, j - n_front)
    w_glr = jnp.pad(w_in_b[:, o_g:o_r], ((0, 0), (0, LANES - rank)))
    w_gu = jnp.pad(w_gate_up, ((0, LANES - rank), (0, 0))).astype(BF16)

    z, glr = _inproj(x2, norm_mix_pre[None, :], w_in_b, w_hi, w_glr, n_lo, out_tile, tm=1024, tn=tn)
    pm = _pool(z, w_pool.astype(BF16), pool_scale[None, :], batch, seq, c_p // pool_w, tt=2048)
    o = _gla(z, glr, w_gu, b_gate[None, :], val_w, batch, seq, (c_q, c_k, c_v), tt=512)
    mixed = _mix(pm, o, z, gla_norm[None, :], w_branch_a.astype(BF16), w_branch_b.astype(BF16),
                 b_branch_gates, c_r, c_gate, tm=512)
    x1 = _outproj(mixed, w_out.astype(BF16), x2, norm_mix_post[None, :], tm=1024)
    return _ffn(x1, norm_ffn_pre[None, :], w_ffn_gate.astype(BF16), w_ffn_up.astype(BF16),
                w_ffn_down.astype(BF16), norm_ffn_post[None, :], tm=1024, tf=512)


def kernel(x, norm_mix_pre, w_in, w_gate_up, b_gate, w_pool, pool_scale, gla_norm, w_branch_a,
           w_branch_b, b_branch_gates, w_out, norm_mix_post, norm_ffn_pre, w_ffn_gate, w_ffn_up,
           w_ffn_down, norm_ffn_post):
    batch, seq, d = x.shape
    x2 = x.reshape(batch * seq, d)
    params = (norm_mix_pre, w_in, w_gate_up, b_gate, w_pool, pool_scale, gla_norm, w_branch_a,
              w_branch_b, b_branch_gates, w_out, norm_mix_post, norm_ffn_pre, w_ffn_gate,
              w_ffn_up, w_ffn_down, norm_ffn_post)
    for layer in range(norm_mix_pre.shape[0]):
        x2 = _layer(x2, batch, seq, *(p[layer] for p in params))
    return x2.reshape(batch, seq, d)
```

```python
import functools

import jax
import jax.numpy as jnp
from jax import lax
from jax.experimental import pallas as pl
from jax.experimental.pallas import tpu as pltpu

F32 = jnp.float32
BF16 = jnp.bfloat16

EPS = 1e-6
POOL_WINDOWS = (2, 4, 8, 16)
POOL_HALO = 16
GLA_HEADS = 4
GLA_GATE_TAU = 16.0
GLA_CHUNK = 128
LOG2E = 1.4426950408889634
SUBLANES = 8
LANES = 128
VMEM_LIMIT = 62 * 1024 * 1024


def _params(sem):
    return pltpu.CompilerParams(dimension_semantics=sem, vmem_limit_bytes=VMEM_LIMIT)


def _rms(x, g):
    ms = jnp.mean(x * x, axis=-1, keepdims=True)
    return x * lax.rsqrt(ms + EPS) * g


def _dot(a, b):
    return jnp.dot(a, b, preferred_element_type=F32)


def _inproj_kernel(x_ref, g_ref, wlo_ref, whi_ref, wglr_ref, z_ref, glr_ref, h_ref, *, n_lo, splits):
    j = pl.program_id(1)

    @pl.when(j == 0)
    def _():
        rb = x_ref.shape[0] // splits
        for r0 in range(0, x_ref.shape[0], rb):
            rows = slice(r0, r0 + rb)
            hb = _rms(x_ref[rows, :], g_ref[...]).astype(BF16)
            h_ref[rows, :] = hb
            glr_ref[rows, :] = _dot(hb, wglr_ref[...])
            z_ref[rows, :] = _dot(hb, wlo_ref[...]).astype(BF16)

    @pl.when((j > 0) & (j < n_lo))
    def _():
        z_ref[...] = _dot(h_ref[...], wlo_ref[...]).astype(BF16)

    @pl.when(j >= n_lo)
    def _():
        z_ref[...] = _dot(h_ref[...], whi_ref[...]).astype(BF16)


def _inproj(x2, g, w_lo, w_hi, w_glr, n_lo, out_tile, tm, tn):
    n, d = x2.shape
    n_hi = w_hi.shape[1] // tn

    def lo_tile(i, j):
        t = jnp.minimum(j, n_lo - 1)
        return jnp.where(i % 2 == 0, t, n_lo - 1 - t)

    def hi_tile(i, j):
        t = jnp.maximum(j - n_lo, 0)
        return jnp.where(i % 2 == 0, t, n_hi - 1 - t)

    def z_tile(i, j):
        return out_tile(jnp.where(j < n_lo, lo_tile(i, j), n_lo + hi_tile(i, j)))

    return pl.pallas_call(
        functools.partial(_inproj_kernel, n_lo=n_lo, splits=4),
        grid=(n // tm, n_lo + n_hi),
        in_specs=[
            pl.BlockSpec((tm, d), lambda i, j: (i, 0)),
            pl.BlockSpec((1, d), lambda i, j: (0, 0)),
            pl.BlockSpec((d, tn), lambda i, j: (0, lo_tile(i, j))),
            pl.BlockSpec((d, tn), lambda i, j: (0, hi_tile(i, j))),
            pl.BlockSpec((d, LANES), lambda i, j: (0, 0)),
        ],
        out_specs=[
            pl.BlockSpec((tm, tn), lambda i, j: (i, z_tile(i, j))),
            pl.BlockSpec((tm, LANES), lambda i, j: (i, 0)),
        ],
        out_shape=[
            jax.ShapeDtypeStruct((n, (n_lo + n_hi) * tn), BF16),
            jax.ShapeDtypeStruct((n, LANES), F32),
        ],
        scratch_shapes=[pltpu.VMEM((tm, d), BF16)],
        compiler_params=_params(("parallel", "arbitrary")),
        name="in_proj",
    )(x2, g, w_lo, w_hi, w_glr)


def _pool_kernel(p_ref, wp_ref, sc_ref, o_ref, carry_ref, *, tt):
    t = pl.program_id(1)

    @pl.when(t == 0)
    def _():
        carry_ref[...] = jnp.zeros_like(carry_ref)

    p = p_ref[...].astype(F32)
    ext = jnp.concatenate([carry_ref[...], p], axis=0)
    carry_ref[...] = p[tt - POOL_HALO:, :]
    pos = t * tt + lax.broadcasted_iota(jnp.int32, (tt, 1), 0)
    gd = wp_ref.shape[1]
    outs = []
    for g, w in enumerate(POOL_WINDOWS):
        s = ext[:, g * gd:(g + 1) * gd]
        shift = 1
        while shift < w:
            s = s + pltpu.roll(s, shift, 0)
            shift *= 2
        count = jnp.minimum(pos + 1, w).astype(F32)
        d = s[POOL_HALO:, :] / count - p[:, g * gd:(g + 1) * gd]
        y = _dot(d.astype(BF16), wp_ref[g])
        outs.append(y * sc_ref[:, g * gd:(g + 1) * gd])
    o_ref[...] = jnp.concatenate(outs, axis=1).astype(BF16)


def _pool(z, w_pool, scale, batch, seq, col_block, tt):
    width = w_pool.shape[0] * w_pool.shape[1]
    nt = seq // tt
    return pl.pallas_call(
        functools.partial(_pool_kernel, tt=tt),
        grid=(batch, nt),
        in_specs=[
            pl.BlockSpec((tt, width), lambda b, t: (b * nt + t, col_block)),
            pl.BlockSpec(w_pool.shape, lambda b, t: (0, 0, 0)),
            pl.BlockSpec((1, width), lambda b, t: (0, 0)),
        ],
        out_specs=pl.BlockSpec((tt, width), lambda b, t: (b * nt + t, 0)),
        out_shape=jax.ShapeDtypeStruct((batch * seq, width), BF16),
        scratch_shapes=[pltpu.VMEM((POOL_HALO, width), F32)],
        compiler_params=_params(("parallel", "arbitrary")),
        name="pool_mixer",
    )(z, w_pool, scale)


_NT = (((1,), (1,)), ((), ()))
_TN = (((0,), (0,)), ((), ()))


def _store_tiles(ref, x):
    for j in range(ref.shape[0]):
        ref[j] = x[:, j * LANES:(j + 1) * LANES]


def _rows(ref, r0, n):
    return jnp.concatenate([ref[j, pl.ds(r0, n), :] for j in range(ref.shape[0])], axis=1)


def _bcast_row(ref, r, n):
    return jnp.concatenate([ref[j, pl.ds(r, n, stride=0), :] for j in range(ref.shape[0])], axis=1)


def _gla_kernel(q_ref, k_ref, v_ref, glr_ref, wgu_ref, bg_ref,
                o_ref, s_ref, g_scr, q_scr, k_scr, *, tt, scale):
    c = GLA_CHUNK
    kw = q_ref.shape[1]
    dk = kw // GLA_HEADS
    dv = v_ref.shape[1] // GLA_HEADS
    heads = range(GLA_HEADS)

    @pl.when(pl.program_id(1) == 0)
    def _():
        s_ref[...] = jnp.zeros_like(s_ref)

    row = lax.broadcasted_iota(jnp.int32, (c, c), 0)
    col = lax.broadcasted_iota(jnp.int32, (c, c), 1)
    tri = (row >= col).astype(BF16)
    sel_r = lax.broadcasted_iota(jnp.int32, (LANES, c), 0)
    sel_c = lax.broadcasted_iota(jnp.int32, (LANES, c), 1)
    sel = (sel_r == (sel_c % SUBLANES)).astype(BF16)
    lane = lax.broadcasted_iota(jnp.int32, (SUBLANES, LANES), 1)
    sub = lax.broadcasted_iota(jnp.int32, (SUBLANES, LANES), 0)
    slot_mask = [(lane == s) & (sub >= s) for s in range(SUBLANES)]
    in_group = ((row // SUBLANES) == (col // SUBLANES)) & (row >= col)
    level_sizes = []
    b = c // 2
    while b >= 2 * SUBLANES:
        level_sizes.append(b)
        b //= 2
    same_block = [(row // b) == (col // b) for b in level_sizes]

    for ci in range(tt // c):
        rows = slice(ci * c, (ci + 1) * c)
        u = _dot(glr_ref[rows, :].astype(BF16), wgu_ref[...]) + bg_ref[...]
        la = (jnp.minimum(u, 0.0) - jnp.log(1.0 + jnp.exp(-jnp.abs(u)))) * (LOG2E / GLA_GATE_TAU)
        la_hi = la.astype(BF16)
        la_lo = (la - la_hi.astype(F32)).astype(BF16)
        gcum = _dot(tri, la_hi) + _dot(tri, la_lo)
        _store_tiles(g_scr, gcum)
        _store_tiles(q_scr, q_ref[rows, :].astype(F32) * scale)
        _store_tiles(k_scr, k_ref[rows, :].astype(F32))

        def level(b):
            half = b // 2
            qs, ks = [], []
            for s in range(0, c, b):
                gref = _bcast_row(g_scr, s + half, half)
                ks.append((_rows(k_scr, s, half) * jnp.exp2(gref - _rows(g_scr, s, half))).astype(BF16))
                ks.append(jnp.zeros((half, kw), BF16))
                qs.append(jnp.zeros((half, kw), BF16))
                qs.append((_rows(q_scr, s + half, half)
                           * jnp.exp2(_rows(g_scr, s + half, half) - gref)).astype(BF16))
            qb, kb = jnp.concatenate(qs, 0), jnp.concatenate(ks, 0)
            return [lax.dot_general(qb[:, h * dk:(h + 1) * dk], kb[:, h * dk:(h + 1) * dk], _NT,
                                    preferred_element_type=F32) for h in heads]

        a = level(c)
        for b, mask in zip(level_sizes, same_block):
            a = [jnp.where(mask, new, old) for new, old in zip(level(b), a)]

        groups = [[] for _ in heads]
        for g0 in range(0, c, SUBLANES):
            qg, gg = _rows(q_scr, g0, SUBLANES), _rows(g_scr, g0, SUBLANES)
            slots = [jnp.zeros((SUBLANES, LANES), F32) for _ in heads]
            for s in range(SUBLANES):
                p = qg * _bcast_row(k_scr, g0 + s, SUBLANES) * jnp.exp2(gg - _bcast_row(g_scr, g0 + s, SUBLANES))
                for h in heads:
                    score = jnp.sum(p[:, h * dk:(h + 1) * dk], axis=1, keepdims=True)
                    slots[h] = jnp.where(slot_mask[s], score, slots[h])
            for h in heads:
                groups[h].append(slots[h])
        a = [jnp.where(in_group, _dot(jnp.concatenate(groups[h], 0).astype(BF16), sel), a[h]).astype(BF16)
             for h in heads]

        gcum = _rows(g_scr, 0, c)
        g_last8 = _bcast_row(g_scr, c - 1, SUBLANES)
        g_last = jnp.concatenate([g_last8] * (c // SUBLANES), 0)
        qg = (_rows(q_scr, 0, c) * jnp.exp2(gcum)).astype(BF16)
        kd = (_rows(k_scr, 0, c) * jnp.exp2(g_last - gcum)).astype(BF16)
        decay_t = jnp.concatenate([jnp.exp2(g_last8)] * (LANES // SUBLANES), 0)
        for h in heads:
            kc = slice(h * dk, (h + 1) * dk)
            vc = slice(h * dv, (h + 1) * dv)
            v = v_ref[rows, vc]
            s_old = s_ref[h]
            o = _dot(qg[:, kc], s_old.astype(BF16)) + _dot(a[h], v)
            decay = decay_t[:, kc].T
            decay = jnp.concatenate([decay] * (dv // LANES), axis=1)
            s_ref[h] = decay * s_old + lax.dot_general(kd[:, kc], v, _TN, preferred_element_type=F32)
            o_ref[rows, vc] = o.astype(BF16)


def _gla(z, glr, w_gate_up, b_gate, val_w, batch, seq, cols, tt):
    key_w = w_gate_up.shape[1]
    dk = key_w // GLA_HEADS
    dv = val_w // GLA_HEADS
    nt = seq // tt
    q0, k0, v0 = cols
    tiles = key_w // LANES
    return pl.pallas_call(
        functools.partial(_gla_kernel, tt=tt, scale=dk ** -0.5),
        grid=(batch, nt),
        in_specs=[
            pl.BlockSpec((tt, key_w), lambda b, t: (b * nt + t, q0 // key_w)),
            pl.BlockSpec((tt, key_w), lambda b, t: (b * nt + t, k0 // key_w)),
            pl.BlockSpec((tt, val_w), lambda b, t: (b * nt + t, v0 // val_w)),
            pl.BlockSpec((tt, LANES), lambda b, t: (b * nt + t, 0)),
            pl.BlockSpec((LANES, key_w), lambda b, t: (0, 0)),
            pl.BlockSpec((1, key_w), lambda b, t: (0, 0)),
        ],
        out_specs=pl.BlockSpec((tt, val_w), lambda b, t: (b * nt + t, 0)),
        out_shape=jax.ShapeDtypeStruct((batch * seq, val_w), BF16),
        scratch_shapes=[
            pltpu.VMEM((GLA_HEADS, dk, dv), F32),
            pltpu.VMEM((tiles, GLA_CHUNK, LANES), F32),
            pltpu.VMEM((tiles, GLA_CHUNK, LANES), F32),
            pltpu.VMEM((tiles, GLA_CHUNK, LANES), F32),
        ],
        compiler_params=_params(("parallel", "arbitrary")),
        name="gla",
    )(z, z, z, glr, w_gate_up, b_gate)


def _mix_out_kernel(pm_ref, o_ref, r_ref, gn_ref, wa_ref, wb_ref, ga_ref, gb_ref, bias_ref,
                    wo_ref, x_ref, gpost_ref, out_ref):
    dv = gn_ref.shape[1]
    y_b = None
    for h in range(o_ref.shape[1] // dv):
        vc = slice(h * dv, (h + 1) * dv)
        r = r_ref[:, vc].astype(F32)
        on = (_rms(o_ref[:, vc].astype(F32), gn_ref[...]) * (r * jax.nn.sigmoid(r))).astype(BF16)
        part = _dot(on, wb_ref[vc, :])
        y_b = part if y_b is None else y_b + part
    y_a = _dot(pm_ref[...], wa_ref[...])
    gate_a = jax.nn.sigmoid(ga_ref[...].astype(F32) + bias_ref[0:1, :])
    gate_b = jax.nn.sigmoid(gb_ref[...].astype(F32) + bias_ref[1:2, :])
    mixed = (gate_a * y_a + gate_b * y_b).astype(BF16)
    out_ref[...] = x_ref[...] + _rms(_dot(mixed, wo_ref[...]), gpost_ref[...])


def _mix_out(pm, o, z, gla_norm, w_a, w_b, bias, w_o, x2, g_post, r_col, gate_col, tm):
    n, d = x2.shape
    val_w = o.shape[1]
    resident = dict(pipeline_mode=pl.Buffered(1))
    return pl.pallas_call(
        _mix_out_kernel,
        grid=(n // tm,),
        in_specs=[
            pl.BlockSpec((tm, pm.shape[1]), lambda i: (i, 0)),
            pl.BlockSpec((tm, val_w), lambda i: (i, 0)),
            pl.BlockSpec((tm, val_w), lambda i: (i, r_col // val_w)),
            pl.BlockSpec((1, gla_norm.shape[1]), lambda i: (0, 0)),
            pl.BlockSpec(w_a.shape, lambda i: (0, 0), **resident),
            pl.BlockSpec(w_b.shape, lambda i: (0, 0), **resident),
            pl.BlockSpec((tm, d), lambda i: (i, gate_col // d)),
            pl.BlockSpec((tm, d), lambda i: (i, gate_col // d + 1)),
            pl.BlockSpec((2, d), lambda i: (0, 0)),
            pl.BlockSpec(w_o.shape, lambda i: (0, 0), **resident),
            pl.BlockSpec((tm, d), lambda i: (i, 0)),
            pl.BlockSpec((1, d), lambda i: (0, 0)),
        ],
        out_specs=pl.BlockSpec((tm, d), lambda i: (i, 0)),
        out_shape=jax.ShapeDtypeStruct((n, d), F32),
        compiler_params=_params(("parallel",)),
        name="mix_out",
    )(pm, o, z, gla_norm, w_a, w_b, z, z, bias, w_o, x2, g_post)


def _ffn_kernel(x_ref, gpre_ref, wg_ref, wu_ref, wd_ref, gpost_ref, o_ref, h_ref, acc_ref, *, splits):
    j = pl.program_id(1)
    last = pl.num_programs(1) - 1
    rb = x_ref.shape[0] // splits
    blocks = [slice(r0, r0 + rb) for r0 in range(0, x_ref.shape[0], rb)]

    def hidden_tile(h):
        gate = _dot(h, wg_ref[...])
        up = _dot(h, wu_ref[...])
        act = (gate * jax.nn.sigmoid(gate) * up).astype(BF16)
        return _dot(act, wd_ref[...])

    @pl.when(j == 0)
    def _():
        for rows in blocks:
            h = _rms(x_ref[rows, :], gpre_ref[...]).astype(BF16)
            h_ref[rows, :] = h
            acc_ref[rows, :] = hidden_tile(h)

    @pl.when((j > 0) & (j < last))
    def _():
        acc_ref[...] += hidden_tile(h_ref[...])

    @pl.when(j == last)
    def _():
        for rows in blocks:
            f = acc_ref[rows, :] + hidden_tile(h_ref[rows, :])
            o_ref[rows, :] = x_ref[rows, :] + _rms(f, gpost_ref[...])


def _ffn(x1, g_pre, w_gate, w_up, w_down, g_post, tm, tf):
    n, d = x1.shape
    dff = w_gate.shape[1]
    nf = dff // tf
    ft = lambda i, j: jnp.where(i % 2 == 0, j, nf - 1 - j)
    return pl.pallas_call(
        functools.partial(_ffn_kernel, splits=2),
        grid=(n // tm, nf),
        in_specs=[
            pl.BlockSpec((tm, d), lambda i, j: (i, 0)),
            pl.BlockSpec((1, d), lambda i, j: (0, 0)),
            pl.BlockSpec((d, tf), lambda i, j: (0, ft(i, j))),
            pl.BlockSpec((d, tf), lambda i, j: (0, ft(i, j))),
            pl.BlockSpec((tf, d), lambda i, j: (ft(i, j), 0)),
            pl.BlockSpec((1, d), lambda i, j: (0, 0)),
        ],
        out_specs=pl.BlockSpec((tm, d), lambda i, j: (i, 0)),
        out_shape=jax.ShapeDtypeStruct((n, d), F32),
        scratch_shapes=[pltpu.VMEM((tm, d), BF16), pltpu.VMEM((tm, d), F32)],
        compiler_params=_params(("parallel", "arbitrary")),
        name="ffn",
    )(x1, g_pre, w_gate, w_up, w_down, g_post)


def _layer(x2, batch, seq, norm_mix_pre, w_in, w_gate_up, b_gate, w_pool, pool_scale, gla_norm,
           w_branch_a, w_branch_b, b_branch_gates, w_out, norm_mix_post,
           norm_ffn_pre, w_ffn_gate, w_ffn_up, w_ffn_down, norm_ffn_post):
    d = x2.shape[1]
    pool_w = w_branch_a.shape[0]
    key_w = w_gate_up.shape[1]
    val_w = w_branch_b.shape[0]
    rank = w_gate_up.shape[0]

    o_p, o_q, o_k = 0, pool_w, pool_w + key_w
    o_v = o_k + key_w
    o_g = o_v + val_w
    o_r = o_g + rank
    o_gate = o_r + val_w
    tn = 1024
    w_in_b = w_in.astype(BF16)
    w_hi = w_in_b[:, o_r:]
    n_lo = o_g // tn
    n_front = o_v // tn
    c_v, c_r, c_gate = 0, val_w, 2 * val_w
    c_p = c_gate + 2 * d
    c_q, c_k = c_p + pool_w, c_p + pool_w + key_w
    out_tile = lambda j: jnp.where(j < n_front, j + c_p // tn, j - n_front)
    w_glr = jnp.pad(w_in_b[:, o_g:o_r], ((0, 0), (0, LANES - rank)))
    w_gu = jnp.pad(w_gate_up, ((0, LANES - rank), (0, 0))).astype(BF16)

    z, glr = _inproj(x2, norm_mix_pre[None, :], w_in_b, w_hi, w_glr, n_lo, out_tile, tm=1024, tn=tn)
    pm = _pool(z, w_pool.astype(BF16), pool_scale[None, :], batch, seq, c_p // pool_w, tt=2048)
    o = _gla(z, glr, w_gu, b_gate[None, :], val_w, batch, seq, (c_q, c_k, c_v), tt=512)
    x1 = _mix_out(pm, o, z, gla_norm[None, :], w_branch_a.astype(BF16), w_branch_b.astype(BF16),
                  b_branch_gates, w_out.astype(BF16), x2, norm_mix_post[None, :], c_r, c_gate, tm=512)
    return _ffn(x1, norm_ffn_pre[None, :], w_ffn_gate.astype(BF16), w_ffn_up.astype(BF16),
                w_ffn_down.astype(BF16), norm_ffn_post[None, :], tm=1024, tf=512)


def kernel(x, norm_mix_pre, w_in, w_gate_up, b_gate, w_pool, pool_scale, gla_norm, w_branch_a,
           w_branch_b, b_branch_gates, w_out, norm_mix_post, norm_ffn_pre, w_ffn_gate, w_ffn_up,
           w_ffn_down, norm_ffn_post):
    batch, seq, d = x.shape
    x2 = x.reshape(batch * seq, d)
    params = (norm_mix_pre, w_in, w_gate_up, b_gate, w_pool, pool_scale, gla_norm, w_branch_a,
              w_branch_b, b_branch_gates, w_out, norm_mix_post, norm_ffn_pre, w_ffn_gate,
              w_ffn_up, w_ffn_down, norm_ffn_post)
    for layer in range(norm_mix_pre.shape[0]):
        x2 = _layer(x2, batch, seq, *(p[layer] for p in params))
    return x2.reshape(batch, seq, d)
```

```python
import functools

import jax
import jax.numpy as jnp
from jax import lax
from jax.experimental import pallas as pl
from jax.experimental.pallas import tpu as pltpu

F32 = jnp.float32
BF16 = jnp.bfloat16

EPS = 1e-6
POOL_WINDOWS = (2, 4, 8, 16)
POOL_HALO = 16
GLA_HEADS = 4
GLA_GATE_TAU = 16.0
GLA_CHUNK = 128
LOG2E = 1.4426950408889634
SUBLANES = 8
LANES = 128
VMEM_LIMIT = 62 * 1024 * 1024


def _params(sem):
    return pltpu.CompilerParams(dimension_semantics=sem, vmem_limit_bytes=VMEM_LIMIT)


def _rms(x, g):
    ms = jnp.mean(x * x, axis=-1, keepdims=True)
    return x * lax.rsqrt(ms + EPS) * g


def _dot(a, b):
    return jnp.dot(a, b, preferred_element_type=F32)


def _inproj_kernel(x_ref, g_ref, wlo_ref, whi_ref, wglr_ref, zlo_ref, zhi_ref, glr_ref, h_ref,
                   *, n_lo, splits):
    j = pl.program_id(1)

    @pl.when(j == 0)
    def _():
        rb = x_ref.shape[0] // splits
        for r0 in range(0, x_ref.shape[0], rb):
            rows = slice(r0, r0 + rb)
            hb = _rms(x_ref[rows, :], g_ref[...]).astype(BF16)
            h_ref[rows, :] = hb
            glr_ref[rows, :] = _dot(hb, wglr_ref[...])
            zlo_ref[rows, :] = _dot(hb, wlo_ref[...]).astype(BF16)

    @pl.when((j > 0) & (j < n_lo))
    def _():
        zlo_ref[...] = _dot(h_ref[...], wlo_ref[...]).astype(BF16)

    @pl.when(j >= n_lo)
    def _():
        zhi_ref[...] = _dot(h_ref[...], whi_ref[...]).astype(BF16)


def _inproj(x2, g, w_lo, w_hi, w_glr, n_lo, lo_out_tile, tm, tn_lo, tn_hi):
    n, d = x2.shape
    n_hi = w_hi.shape[1] // tn_hi

    def lo_tile(i, j):
        t = jnp.minimum(j, n_lo - 1)
        return jnp.where(i % 2 == 0, t, n_lo - 1 - t)

    def hi_tile(i, j):
        t = jnp.maximum(j - n_lo, 0)
        return jnp.where(i % 2 == 0, t, n_hi - 1 - t)

    return pl.pallas_call(
        functools.partial(_inproj_kernel, n_lo=n_lo, splits=4),
        grid=(n // tm, n_lo + n_hi),
        in_specs=[
            pl.BlockSpec((tm, d), lambda i, j: (i, 0)),
            pl.BlockSpec((1, d), lambda i, j: (0, 0)),
            pl.BlockSpec((d, tn_lo), lambda i, j: (0, lo_tile(i, j))),
            pl.BlockSpec((d, tn_hi), lambda i, j: (0, hi_tile(i, j))),
            pl.BlockSpec((d, LANES), lambda i, j: (0, 0)),
        ],
        out_specs=[
            pl.BlockSpec((tm, tn_lo), lambda i, j: (i, lo_out_tile(lo_tile(i, j)))),
            pl.BlockSpec((tm, tn_hi), lambda i, j: (i, hi_tile(i, j))),
            pl.BlockSpec((tm, LANES), lambda i, j: (i, 0)),
        ],
        out_shape=[
            jax.ShapeDtypeStruct((n, n_lo * tn_lo), BF16),
            jax.ShapeDtypeStruct((n, n_hi * tn_hi), BF16),
            jax.ShapeDtypeStruct((n, LANES), F32),
        ],
        scratch_shapes=[pltpu.VMEM((tm, d), BF16)],
        compiler_params=_params(("parallel", "arbitrary")),
        name="in_proj",
    )(x2, g, w_lo, w_hi, w_glr)


def _pool_kernel(p_ref, wp_ref, sc_ref, o_ref, carry_ref, *, tt):
    t = pl.program_id(1)

    @pl.when(t == 0)
    def _():
        carry_ref[...] = jnp.zeros_like(carry_ref)

    p = p_ref[...].astype(F32)
    ext = jnp.concatenate([carry_ref[...], p], axis=0)
    carry_ref[...] = p[tt - POOL_HALO:, :]
    pos = t * tt + lax.broadcasted_iota(jnp.int32, (tt, 1), 0)
    gd = wp_ref.shape[1]
    outs = []
    for g, w in enumerate(POOL_WINDOWS):
        s = ext[:, g * gd:(g + 1) * gd]
        shift = 1
        while shift < w:
            s = s + pltpu.roll(s, shift, 0)
            shift *= 2
        count = jnp.minimum(pos + 1, w).astype(F32)
        d = s[POOL_HALO:, :] / count - p[:, g * gd:(g + 1) * gd]
        y = _dot(d.astype(BF16), wp_ref[g])
        outs.append(y * sc_ref[:, g * gd:(g + 1) * gd])
    o_ref[...] = jnp.concatenate(outs, axis=1).astype(BF16)


def _pool(z, w_pool, scale, batch, seq, col_block, tt):
    width = w_pool.shape[0] * w_pool.shape[1]
    nt = seq // tt
    return pl.pallas_call(
        functools.partial(_pool_kernel, tt=tt),
        grid=(batch, nt),
        in_specs=[
            pl.BlockSpec((tt, width), lambda b, t: (b * nt + t, col_block)),
            pl.BlockSpec(w_pool.shape, lambda b, t: (0, 0, 0)),
            pl.BlockSpec((1, width), lambda b, t: (0, 0)),
        ],
        out_specs=pl.BlockSpec((tt, width), lambda b, t: (b * nt + t, 0)),
        out_shape=jax.ShapeDtypeStruct((batch * seq, width), BF16),
        scratch_shapes=[pltpu.VMEM((POOL_HALO, width), F32)],
        compiler_params=_params(("parallel", "arbitrary")),
        name="pool_mixer",
    )(z, w_pool, scale)


_NT = (((1,), (1,)), ((), ()))
_TN = (((0,), (0,)), ((), ()))


def _store_tiles(ref, x):
    for j in range(ref.shape[0]):
        ref[j] = x[:, j * LANES:(j + 1) * LANES]


def _rows(ref, r0, n):
    return jnp.concatenate([ref[j, pl.ds(r0, n), :] for j in range(ref.shape[0])], axis=1)


def _bcast_row(ref, r, n):
    return jnp.concatenate([ref[j, pl.ds(r, n, stride=0), :] for j in range(ref.shape[0])], axis=1)


def _gla_kernel(q_ref, k_ref, v_ref, glr_ref, wgu_ref, bg_ref,
                o_ref, s_ref, g_scr, q_scr, k_scr, *, tt, scale):
    c = GLA_CHUNK
    kw = q_ref.shape[1]
    dk = kw // GLA_HEADS
    dv = v_ref.shape[1] // GLA_HEADS
    heads = range(GLA_HEADS)

    @pl.when(pl.program_id(1) == 0)
    def _():
        s_ref[...] = jnp.zeros_like(s_ref)

    row = lax.broadcasted_iota(jnp.int32, (c, c), 0)
    col = lax.broadcasted_iota(jnp.int32, (c, c), 1)
    tri = (row >= col).astype(BF16)
    sel_r = lax.broadcasted_iota(jnp.int32, (LANES, c), 0)
    sel_c = lax.broadcasted_iota(jnp.int32, (LANES, c), 1)
    sel = (sel_r == (sel_c % SUBLANES)).astype(BF16)
    lane = lax.broadcasted_iota(jnp.int32, (SUBLANES, LANES), 1)
    sub = lax.broadcasted_iota(jnp.int32, (SUBLANES, LANES), 0)
    slot_mask = [(lane == s) & (sub >= s) for s in range(SUBLANES)]
    in_group = ((row // SUBLANES) == (col // SUBLANES)) & (row >= col)
    level_sizes = []
    b = c // 2
    while b >= 2 * SUBLANES:
        level_sizes.append(b)
        b //= 2
    same_block = [(row // b) == (col // b) for b in level_sizes]

    for ci in range(tt // c):
        rows = slice(ci * c, (ci + 1) * c)
        u = _dot(glr_ref[rows, :].astype(BF16), wgu_ref[...]) + bg_ref[...]
        la = (jnp.minimum(u, 0.0) - jnp.log(1.0 + jnp.exp(-jnp.abs(u)))) * (LOG2E / GLA_GATE_TAU)
        la_hi = la.astype(BF16)
        la_lo = (la - la_hi.astype(F32)).astype(BF16)
        gcum = _dot(tri, la_hi) + _dot(tri, la_lo)
        _store_tiles(g_scr, gcum)
        _store_tiles(q_scr, q_ref[rows, :].astype(F32) * scale)
        _store_tiles(k_scr, k_ref[rows, :].astype(F32))

        def level(b):
            half = b // 2
            qs, ks = [], []
            for s in range(0, c, b):
                gref = _bcast_row(g_scr, s + half, half)
                ks.append((_rows(k_scr, s, half) * jnp.exp2(gref - _rows(g_scr, s, half))).astype(BF16))
                ks.append(jnp.zeros((half, kw), BF16))
                qs.append(jnp.zeros((half, kw), BF16))
                qs.append((_rows(q_scr, s + half, half)
                           * jnp.exp2(_rows(g_scr, s + half, half) - gref)).astype(BF16))
            qb, kb = jnp.concatenate(qs, 0), jnp.concatenate(ks, 0)
            return [lax.dot_general(qb[:, h * dk:(h + 1) * dk], kb[:, h * dk:(h + 1) * dk], _NT,
                                    preferred_element_type=F32) for h in heads]

        a = level(c)
        for b, mask in zip(level_sizes, same_block):
            a = [jnp.where(mask, new, old) for new, old in zip(level(b), a)]

        groups = [[] for _ in heads]
        for g0 in range(0, c, SUBLANES):
            qg, gg = _rows(q_scr, g0, SUBLANES), _rows(g_scr, g0, SUBLANES)
            slots = [jnp.zeros((SUBLANES, LANES), F32) for _ in heads]
            for s in range(SUBLANES):
                p = qg * _bcast_row(k_scr, g0 + s, SUBLANES) * jnp.exp2(gg - _bcast_row(g_scr, g0 + s, SUBLANES))
                for h in heads:
                    score = jnp.sum(p[:, h * dk:(h + 1) * dk], axis=1, keepdims=True)
                    slots[h] = jnp.where(slot_mask[s], score, slots[h])
            for h in heads:
                groups[h].append(slots[h])
        a = [jnp.where(in_group, _dot(jnp.concatenate(groups[h], 0).astype(BF16), sel), a[h]).astype(BF16)
             for h in heads]

        gcum = _rows(g_scr, 0, c)
        g_last8 = _bcast_row(g_scr, c - 1, SUBLANES)
        g_last = jnp.concatenate([g_last8] * (c // SUBLANES), 0)
        qg = (_rows(q_scr, 0, c) * jnp.exp2(gcum)).astype(BF16)
        kd = (_rows(k_scr, 0, c) * jnp.exp2(g_last - gcum)).astype(BF16)
        decay_t = jnp.concatenate([jnp.exp2(g_last8)] * (LANES // SUBLANES), 0)
        for h in heads:
            kc = slice(h * dk, (h + 1) * dk)
            vc = slice(h * dv, (h + 1) * dv)
            v = v_ref[rows, vc]
            s_old = s_ref[h]
            o = _dot(qg[:, kc], s_old.astype(BF16)) + _dot(a[h], v)
            decay = decay_t[:, kc].T
            decay = jnp.concatenate([decay] * (dv // LANES), axis=1)
            s_ref[h] = decay * s_old + lax.dot_general(kd[:, kc], v, _TN, preferred_element_type=F32)
            o_ref[rows, vc] = o.astype(BF16)


def _gla(z, glr, w_gate_up, b_gate, val_w, batch, seq, cols, tt):
    key_w = w_gate_up.shape[1]
    dk = key_w // GLA_HEADS
    dv = val_w // GLA_HEADS
    nt = seq // tt
    q0, k0, v0 = cols
    tiles = key_w // LANES
    return pl.pallas_call(
        functools.partial(_gla_kernel, tt=tt, scale=dk ** -0.5),
        grid=(batch, nt),
        in_specs=[
            pl.BlockSpec((tt, key_w), lambda b, t: (b * nt + t, q0 // key_w)),
            pl.BlockSpec((tt, key_w), lambda b, t: (b * nt + t, k0 // key_w)),
            pl.BlockSpec((tt, val_w), lambda b, t: (b * nt + t, v0 // val_w)),
            pl.BlockSpec((tt, LANES), lambda b, t: (b * nt + t, 0)),
            pl.BlockSpec((LANES, key_w), lambda b, t: (0, 0)),
            pl.BlockSpec((1, key_w), lambda b, t: (0, 0)),
        ],
        out_specs=pl.BlockSpec((tt, val_w), lambda b, t: (b * nt + t, 0)),
        out_shape=jax.ShapeDtypeStruct((batch * seq, val_w), BF16),
        scratch_shapes=[
            pltpu.VMEM((GLA_HEADS, dk, dv), F32),
            pltpu.VMEM((tiles, GLA_CHUNK, LANES), F32),
            pltpu.VMEM((tiles, GLA_CHUNK, LANES), F32),
            pltpu.VMEM((tiles, GLA_CHUNK, LANES), F32),
        ],
        compiler_params=_params(("parallel", "arbitrary")),
        name="gla",
    )(z, z, z, glr, w_gate_up, b_gate)


def _mix_out_kernel(pm_ref, o_ref, r_ref, gn_ref, wa_ref, wb_ref, ga_ref, gb_ref, bias_ref,
                    wo_ref, x_ref, gpost_ref, out_ref):
    dv = gn_ref.shape[1]
    y_b = None
    for h in range(o_ref.shape[1] // dv):
        vc = slice(h * dv, (h + 1) * dv)
        r = r_ref[:, vc].astype(F32)
        on = (_rms(o_ref[:, vc].astype(F32), gn_ref[...]) * (r * jax.nn.sigmoid(r))).astype(BF16)
        part = _dot(on, wb_ref[vc, :])
        y_b = part if y_b is None else y_b + part
    y_a = _dot(pm_ref[...], wa_ref[...])
    gate_a = jax.nn.sigmoid(ga_ref[...].astype(F32) + bias_ref[0:1, :])
    gate_b = jax.nn.sigmoid(gb_ref[...].astype(F32) + bias_ref[1:2, :])
    mixed = (gate_a * y_a + gate_b * y_b).astype(BF16)
    out_ref[...] = x_ref[...] + _rms(_dot(mixed, wo_ref[...]), gpost_ref[...])


def _mix_out(pm, o, z, gla_norm, w_a, w_b, bias, w_o, x2, g_post, r_col, gate_col, tm):
    n, d = x2.shape
    val_w = o.shape[1]
    resident = dict(pipeline_mode=pl.Buffered(1))
    return pl.pallas_call(
        _mix_out_kernel,
        grid=(n // tm,),
        in_specs=[
            pl.BlockSpec((tm, pm.shape[1]), lambda i: (i, 0)),
            pl.BlockSpec((tm, val_w), lambda i: (i, 0)),
            pl.BlockSpec((tm, val_w), lambda i: (i, r_col // val_w)),
            pl.BlockSpec((1, gla_norm.shape[1]), lambda i: (0, 0)),
            pl.BlockSpec(w_a.shape, lambda i: (0, 0), **resident),
            pl.BlockSpec(w_b.shape, lambda i: (0, 0), **resident),
            pl.BlockSpec((tm, d), lambda i: (i, gate_col // d)),
            pl.BlockSpec((tm, d), lambda i: (i, gate_col // d + 1)),
            pl.BlockSpec((2, d), lambda i: (0, 0)),
            pl.BlockSpec(w_o.shape, lambda i: (0, 0), **resident),
            pl.BlockSpec((tm, d), lambda i: (i, 0)),
            pl.BlockSpec((1, d), lambda i: (0, 0)),
        ],
        out_specs=pl.BlockSpec((tm, d), lambda i: (i, 0)),
        out_shape=jax.ShapeDtypeStruct((n, d), F32),
        compiler_params=_params(("parallel",)),
        name="mix_out",
    )(pm, o, z, gla_norm, w_a, w_b, z, z, bias, w_o, x2, g_post)


def _ffn_kernel(x_ref, gpre_ref, wg_ref, wu_ref, wd_ref, gpost_ref, o_ref, h_ref, acc_ref, *, splits):
    j = pl.program_id(1)
    last = pl.num_programs(1) - 1
    rb = x_ref.shape[0] // splits
    blocks = [slice(r0, r0 + rb) for r0 in range(0, x_ref.shape[0], rb)]

    def hidden_tile(h):
        gate = _dot(h, wg_ref[...])
        up = _dot(h, wu_ref[...])
        act = (gate * jax.nn.sigmoid(gate) * up).astype(BF16)
        return _dot(act, wd_ref[...])

    @pl.when(j == 0)
    def _():
        for rows in blocks:
            h = _rms(x_ref[rows, :], gpre_ref[...]).astype(BF16)
            h_ref[rows, :] = h
            acc_ref[rows, :] = hidden_tile(h)

    @pl.when((j > 0) & (j < last))
    def _():
        acc_ref[...] += hidden_tile(h_ref[...])

    @pl.when(j == last)
    def _():
        for rows in blocks:
            f = acc_ref[rows, :] + hidden_tile(h_ref[rows, :])
            o_ref[rows, :] = x_ref[rows, :] + _rms(f, gpost_ref[...])


def _ffn(x1, g_pre, w_gate, w_up, w_down, g_post, tm, tf):
    n, d = x1.shape
    dff = w_gate.shape[1]
    nf = dff // tf
    ft = lambda i, j: jnp.where(i % 2 == 0, j, nf - 1 - j)
    return pl.pallas_call(
        functools.partial(_ffn_kernel, splits=2),
        grid=(n // tm, nf),
        in_specs=[
            pl.BlockSpec((tm, d), lambda i, j: (i, 0)),
            pl.BlockSpec((1, d), lambda i, j: (0, 0)),
            pl.BlockSpec((d, tf), lambda i, j: (0, ft(i, j))),
            pl.BlockSpec((d, tf), lambda i, j: (0, ft(i, j))),
            pl.BlockSpec((tf, d), lambda i, j: (ft(i, j), 0)),
            pl.BlockSpec((1, d), lambda i, j: (0, 0)),
        ],
        out_specs=pl.BlockSpec((tm, d), lambda i, j: (i, 0)),
        out_shape=jax.ShapeDtypeStruct((n, d), F32),
        scratch_shapes=[pltpu.VMEM((tm, d), BF16), pltpu.VMEM((tm, d), F32)],
        compiler_params=_params(("parallel", "arbitrary")),
        name="ffn",
    )(x1, g_pre, w_gate, w_up, w_down, g_post)


def _layer(x2, batch, seq, norm_mix_pre, w_in, w_gate_up, b_gate, w_pool, pool_scale, gla_norm,
           w_branch_a, w_branch_b, b_branch_gates, w_out, norm_mix_post,
           norm_ffn_pre, w_ffn_gate, w_ffn_up, w_ffn_down, norm_ffn_post):
    d = x2.shape[1]
    pool_w = w_branch_a.shape[0]
    key_w = w_gate_up.shape[1]
    val_w = w_branch_b.shape[0]
    rank = w_gate_up.shape[0]

    o_v = pool_w + 2 * key_w
    o_g = o_v + val_w
    o_r = o_g + rank
    tn_lo = 1024
    w_in_b = w_in.astype(BF16)
    w_hi = w_in_b[:, o_r:]
    n_lo = o_g // tn_lo
    n_front = o_v // tn_lo
    c_v, c_p = 0, val_w
    c_q, c_k = c_p + pool_w, c_p + pool_w + key_w
    lo_out_tile = lambda s: jnp.where(s < n_front, s + val_w // tn_lo, s - n_front)
    c_r, c_gate = 0, val_w
    w_glr = jnp.pad(w_in_b[:, o_g:o_r], ((0, 0), (0, LANES - rank)))
    w_gu = jnp.pad(w_gate_up, ((0, LANES - rank), (0, 0))).astype(BF16)

    z_lo, z_hi, glr = _inproj(x2, norm_mix_pre[None, :], w_in_b, w_hi, w_glr, n_lo, lo_out_tile,
                              tm=1024, tn_lo=tn_lo, tn_hi=2048)
    pm = _pool(z_lo, w_pool.astype(BF16), pool_scale[None, :], batch, seq, c_p // pool_w, tt=2048)
    o = _gla(z_lo, glr, w_gu, b_gate[None, :], val_w, batch, seq, (c_q, c_k, c_v), tt=512)
    x1 = _mix_out(pm, o, z_hi, gla_norm[None, :], w_branch_a.astype(BF16), w_branch_b.astype(BF16),
                  b_branch_gates, w_out.astype(BF16), x2, norm_mix_post[None, :], c_r, c_gate, tm=512)
    return _ffn(x1, norm_ffn_pre[None, :], w_ffn_gate.astype(BF16), w_ffn_up.astype(BF16),
                w_ffn_down.astype(BF16), norm_ffn_post[None, :], tm=1024, tf=512)


def kernel(x, norm_mix_pre, w_in, w_gate_up, b_gate, w_pool, pool_scale, gla_norm, w_branch_a,
           w_branch_b, b_branch_gates, w_out, norm_mix_post, norm_ffn_pre, w_ffn_gate, w_ffn_up,
           w_ffn_down, norm_ffn_post):
    batch, seq, d = x.shape
    x2 = x.reshape(batch * seq, d)
    params = (norm_mix_pre, w_in, w_gate_up, b_gate, w_pool, pool_scale, gla_norm, w_branch_a,
              w_branch_b, b_branch_gates, w_out, norm_mix_post, norm_ffn_pre, w_ffn_gate,
              w_ffn_up, w_ffn_down, norm_ffn_post)
    for layer in range(norm_mix_pre.shape[0]):
        x2 = _layer(x2, batch, seq, *(p[layer] for p in params))
    return x2.reshape(batch, seq, d)
```

```python
import functools

import jax
import jax.numpy as jnp
from jax import lax
from jax.experimental import pallas as pl
from jax.experimental.pallas import tpu as pltpu

F32 = jnp.float32
BF16 = jnp.bfloat16

EPS = 1e-6
POOL_WINDOWS = (2, 4, 8, 16)
POOL_HALO = 16
GLA_HEADS = 4
GLA_GATE_TAU = 16.0
GLA_CHUNK = 128
LOG2E = 1.4426950408889634
SUBLANES = 8
LANES = 128
VMEM_LIMIT = 62 * 1024 * 1024


def _params(sem):
    return pltpu.CompilerParams(dimension_semantics=sem, vmem_limit_bytes=VMEM_LIMIT)


def _rms(x, g):
    ms = jnp.mean(x * x, axis=-1, keepdims=True)
    return x * lax.rsqrt(ms + EPS) * g


def _dot(a, b):
    return jnp.dot(a, b, preferred_element_type=F32)


def _inproj_kernel(x_ref, g_ref, wlo_ref, whi_ref, wglr_ref, zlo_ref, zhi_ref, glr_ref, h_ref,
                   *, n_lo, splits):
    j = pl.program_id(1)

    @pl.when(j == 0)
    def _():
        rb = x_ref.shape[0] // splits
        for r0 in range(0, x_ref.shape[0], rb):
            rows = slice(r0, r0 + rb)
            hb = _rms(x_ref[rows, :], g_ref[...]).astype(BF16)
            h_ref[rows, :] = hb
            glr_ref[rows, :] = _dot(hb, wglr_ref[...])
            zlo_ref[rows, :] = _dot(hb, wlo_ref[...]).astype(BF16)

    @pl.when((j > 0) & (j < n_lo))
    def _():
        zlo_ref[...] = _dot(h_ref[...], wlo_ref[...]).astype(BF16)

    @pl.when(j >= n_lo)
    def _():
        zhi_ref[...] = _dot(h_ref[...], whi_ref[...]).astype(BF16)


def _inproj(x2, g, w_lo, w_hi, w_glr, n_lo, lo_out_tile, tm, tn_lo, tn_hi):
    n, d = x2.shape
    n_hi = w_hi.shape[1] // tn_hi

    def lo_tile(i, j):
        t = jnp.minimum(j, n_lo - 1)
        return jnp.where(i % 2 == 0, t, n_lo - 1 - t)

    def hi_tile(i, j):
        t = jnp.maximum(j - n_lo, 0)
        return jnp.where(i % 2 == 0, t, n_hi - 1 - t)

    return pl.pallas_call(
        functools.partial(_inproj_kernel, n_lo=n_lo, splits=4),
        grid=(n // tm, n_lo + n_hi),
        in_specs=[
            pl.BlockSpec((tm, d), lambda i, j: (i, 0)),
            pl.BlockSpec((1, d), lambda i, j: (0, 0)),
            pl.BlockSpec((d, tn_lo), lambda i, j: (0, lo_tile(i, j))),
            pl.BlockSpec((d, tn_hi), lambda i, j: (0, hi_tile(i, j))),
            pl.BlockSpec((d, LANES), lambda i, j: (0, 0)),
        ],
        out_specs=[
            pl.BlockSpec((tm, tn_lo), lambda i, j: (i, lo_out_tile(lo_tile(i, j)))),
            pl.BlockSpec((tm, tn_hi), lambda i, j: (i, hi_tile(i, j))),
            pl.BlockSpec((tm, LANES), lambda i, j: (i, 0)),
        ],
        out_shape=[
            jax.ShapeDtypeStruct((n, n_lo * tn_lo), BF16),
            jax.ShapeDtypeStruct((n, n_hi * tn_hi), BF16),
            jax.ShapeDtypeStruct((n, LANES), F32),
        ],
        scratch_shapes=[pltpu.VMEM((tm, d), BF16)],
        compiler_params=_params(("parallel", "arbitrary")),
        name="in_proj",
    )(x2, g, w_lo, w_hi, w_glr)


def _pool_kernel(p_ref, wp_ref, sc_ref, o_ref, carry_ref, *, tt):
    t = pl.program_id(1)

    @pl.when(t == 0)
    def _():
        carry_ref[...] = jnp.zeros_like(carry_ref)

    p = p_ref[...].astype(F32)
    ext = jnp.concatenate([carry_ref[...], p], axis=0)
    carry_ref[...] = p[tt - POOL_HALO:, :]
    pos = t * tt + lax.broadcasted_iota(jnp.int32, (tt, 1), 0)
    gd = wp_ref.shape[1]
    outs = []
    for g, w in enumerate(POOL_WINDOWS):
        s = ext[:, g * gd:(g + 1) * gd]
        shift = 1
        while shift < w:
            s = s + pltpu.roll(s, shift, 0)
            shift *= 2
        count = jnp.minimum(pos + 1, w).astype(F32)
        d = s[POOL_HALO:, :] / count - p[:, g * gd:(g + 1) * gd]
        y = _dot(d.astype(BF16), wp_ref[g])
        outs.append(y * sc_ref[:, g * gd:(g + 1) * gd])
    o_ref[...] = jnp.concatenate(outs, axis=1).astype(BF16)


def _pool(z, w_pool, scale, batch, seq, col_block, tt):
    width = w_pool.shape[0] * w_pool.shape[1]
    nt = seq // tt
    return pl.pallas_call(
        functools.partial(_pool_kernel, tt=tt),
        grid=(batch, nt),
        in_specs=[
            pl.BlockSpec((tt, width), lambda b, t: (b * nt + t, col_block)),
            pl.BlockSpec(w_pool.shape, lambda b, t: (0, 0, 0)),
            pl.BlockSpec((1, width), lambda b, t: (0, 0)),
        ],
        out_specs=pl.BlockSpec((tt, width), lambda b, t: (b * nt + t, 0)),
        out_shape=jax.ShapeDtypeStruct((batch * seq, width), BF16),
        scratch_shapes=[pltpu.VMEM((POOL_HALO, width), F32)],
        compiler_params=_params(("parallel", "arbitrary")),
        name="pool_mixer",
    )(z, w_pool, scale)


_NT = (((1,), (1,)), ((), ()))
_TN = (((0,), (0,)), ((), ()))


def _store_tiles(ref, x):
    for j in range(ref.shape[0]):
        ref[j] = x[:, j * LANES:(j + 1) * LANES]


def _rows(ref, r0, n):
    return jnp.concatenate([ref[j, pl.ds(r0, n), :] for j in range(ref.shape[0])], axis=1)


def _bcast_row(ref, r, n):
    return jnp.concatenate([ref[j, pl.ds(r, n, stride=0), :] for j in range(ref.shape[0])], axis=1)


def _gla_kernel(q_ref, k_ref, v_ref, glr_ref, wgu_ref, bg_ref,
                o_ref, s_ref, g_scr, q_scr, k_scr, *, tt, scale):
    c = GLA_CHUNK
    kw = q_ref.shape[1]
    dk = kw // GLA_HEADS
    dv = v_ref.shape[1] // GLA_HEADS
    heads = range(GLA_HEADS)

    @pl.when(pl.program_id(1) == 0)
    def _():
        s_ref[...] = jnp.zeros_like(s_ref)

    row = lax.broadcasted_iota(jnp.int32, (c, c), 0)
    col = lax.broadcasted_iota(jnp.int32, (c, c), 1)
    tri = (row >= col).astype(BF16)
    sel_r = lax.broadcasted_iota(jnp.int32, (LANES, c), 0)
    sel_c = lax.broadcasted_iota(jnp.int32, (LANES, c), 1)
    sel = (sel_r == (sel_c % SUBLANES)).astype(BF16)
    lane = lax.broadcasted_iota(jnp.int32, (SUBLANES, LANES), 1)
    sub = lax.broadcasted_iota(jnp.int32, (SUBLANES, LANES), 0)
    slot_mask = [(lane == s) & (sub >= s) for s in range(SUBLANES)]
    in_group = ((row // SUBLANES) == (col // SUBLANES)) & (row >= col)
    level_sizes = []
    b = c // 2
    while b >= 2 * SUBLANES:
        level_sizes.append(b)
        b //= 2
    same_block = [(row // b) == (col // b) for b in level_sizes]

    for ci in range(tt // c):
        rows = slice(ci * c, (ci + 1) * c)
        u = _dot(glr_ref[rows, :].astype(BF16), wgu_ref[...]) + bg_ref[...]
        la = (jnp.minimum(u, 0.0) - jnp.log(1.0 + jnp.exp(-jnp.abs(u)))) * (LOG2E / GLA_GATE_TAU)
        la_hi = la.astype(BF16)
        la_lo = (la - la_hi.astype(F32)).astype(BF16)
        gcum = _dot(tri, la_hi) + _dot(tri, la_lo)
        _store_tiles(g_scr, gcum)
        _store_tiles(q_scr, q_ref[rows, :].astype(F32) * scale)
        _store_tiles(k_scr, k_ref[rows, :].astype(F32))

        def level(b):
            half = b // 2
            qs, ks = [], []
            for s in range(0, c, b):
                gref = _bcast_row(g_scr, s + half, half)
                ks.append((_rows(k_scr, s, half) * jnp.exp2(gref - _rows(g_scr, s, half))).astype(BF16))
                ks.append(jnp.zeros((half, kw), BF16))
                qs.append(jnp.zeros((half, kw), BF16))
                qs.append((_rows(q_scr, s + half, half)
                           * jnp.exp2(_rows(g_scr, s + half, half) - gref)).astype(BF16))
            qb, kb = jnp.concatenate(qs, 0), jnp.concatenate(ks, 0)
            return [lax.dot_general(qb[:, h * dk:(h + 1) * dk], kb[:, h * dk:(h + 1) * dk], _NT,
                                    preferred_element_type=F32) for h in heads]

        a = level(c)
        for b, mask in zip(level_sizes, same_block):
            a = [jnp.where(mask, new, old) for new, old in zip(level(b), a)]

        groups = [[] for _ in heads]
        for g0 in range(0, c, SUBLANES):
            qg, gg = _rows(q_scr, g0, SUBLANES), _rows(g_scr, g0, SUBLANES)
            slots = [jnp.zeros((SUBLANES, LANES), F32) for _ in heads]
            for s in range(SUBLANES):
                p = qg * _bcast_row(k_scr, g0 + s, SUBLANES) * jnp.exp2(gg - _bcast_row(g_scr, g0 + s, SUBLANES))
                for h in heads:
                    score = jnp.sum(p[:, h * dk:(h + 1) * dk], axis=1, keepdims=True)
                    slots[h] = jnp.where(slot_mask[s], score, slots[h])
            for h in heads:
                groups[h].append(slots[h])
        a = [jnp.where(in_group, _dot(jnp.concatenate(groups[h], 0).astype(BF16), sel), a[h]).astype(BF16)
             for h in heads]

        gcum = _rows(g_scr, 0, c)
        g_last8 = _bcast_row(g_scr, c - 1, SUBLANES)
        g_last = jnp.concatenate([g_last8] * (c // SUBLANES), 0)
        qg = (_rows(q_scr, 0, c) * jnp.exp2(gcum)).astype(BF16)
        kd = (_rows(k_scr, 0, c) * jnp.exp2(g_last - gcum)).astype(BF16)
        decay_t = jnp.concatenate([jnp.exp2(g_last8)] * (LANES // SUBLANES), 0)
        for h in heads:
            kc = slice(h * dk, (h + 1) * dk)
            vc = slice(h * dv, (h + 1) * dv)
            v = v_ref[rows, vc]
            s_old = s_ref[h]
            o = _dot(qg[:, kc], s_old.astype(BF16)) + _dot(a[h], v)
            decay = decay_t[:, kc].T
            decay = jnp.concatenate([decay] * (dv // LANES), axis=1)
            s_ref[h] = decay * s_old + lax.dot_general(kd[:, kc], v, _TN, preferred_element_type=F32)
            o_ref[rows, vc] = o.astype(BF16)


def _gla(z, glr, w_gate_up, b_gate, val_w, batch, seq, cols, tt):
    key_w = w_gate_up.shape[1]
    dk = key_w // GLA_HEADS
    dv = val_w // GLA_HEADS
    nt = seq // tt
    q0, k0, v0 = cols
    tiles = key_w // LANES
    return pl.pallas_call(
        functools.partial(_gla_kernel, tt=tt, scale=dk ** -0.5),
        grid=(batch, nt),
        in_specs=[
            pl.BlockSpec((tt, key_w), lambda b, t: (b * nt + t, q0 // key_w)),
            pl.BlockSpec((tt, key_w), lambda b, t: (b * nt + t, k0 // key_w)),
            pl.BlockSpec((tt, val_w), lambda b, t: (b * nt + t, v0 // val_w)),
            pl.BlockSpec((tt, LANES), lambda b, t: (b * nt + t, 0)),
            pl.BlockSpec((LANES, key_w), lambda b, t: (0, 0)),
            pl.BlockSpec((1, key_w), lambda b, t: (0, 0)),
        ],
        out_specs=pl.BlockSpec((tt, val_w), lambda b, t: (b * nt + t, 0)),
        out_shape=jax.ShapeDtypeStruct((batch * seq, val_w), BF16),
        scratch_shapes=[
            pltpu.VMEM((GLA_HEADS, dk, dv), F32),
            pltpu.VMEM((tiles, GLA_CHUNK, LANES), F32),
            pltpu.VMEM((tiles, GLA_CHUNK, LANES), F32),
            pltpu.VMEM((tiles, GLA_CHUNK, LANES), F32),
        ],
        compiler_params=_params(("parallel", "arbitrary")),
        name="gla",
    )(z, z, z, glr, w_gate_up, b_gate)


def _mix_out_kernel(pm_ref, o_ref, r_ref, gn_ref, wa_ref, wb_ref, ga_ref, gb_ref, bias_ref,
                    wo_ref, x_ref, gpost_ref, out_ref):
    dv = gn_ref.shape[1]
    y_b = None
    for h in range(o_ref.shape[1] // dv):
        vc = slice(h * dv, (h + 1) * dv)
        r = r_ref[:, vc].astype(F32)
        on = (_rms(o_ref[:, vc].astype(F32), gn_ref[...]) * (r * jax.nn.sigmoid(r))).astype(BF16)
        part = _dot(on, wb_ref[vc, :])
        y_b = part if y_b is None else y_b + part
    y_a = _dot(pm_ref[...], wa_ref[...])
    gate_a = jax.nn.sigmoid(ga_ref[...].astype(F32) + bias_ref[0:1, :])
    gate_b = jax.nn.sigmoid(gb_ref[...].astype(F32) + bias_ref[1:2, :])
    mixed = (gate_a * y_a + gate_b * y_b).astype(BF16)
    out_ref[...] = x_ref[...] + _rms(_dot(mixed, wo_ref[...]), gpost_ref[...])


def _mix_out(pm, o, z, gla_norm, w_a, w_b, bias, w_o, x2, g_post, r_col, gate_col, tm):
    n, d = x2.shape
    val_w = o.shape[1]
    resident = dict(pipeline_mode=pl.Buffered(1))
    return pl.pallas_call(
        _mix_out_kernel,
        grid=(n // tm,),
        in_specs=[
            pl.BlockSpec((tm, pm.shape[1]), lambda i: (i, 0)),
            pl.BlockSpec((tm, val_w), lambda i: (i, 0)),
            pl.BlockSpec((tm, val_w), lambda i: (i, r_col // val_w)),
            pl.BlockSpec((1, gla_norm.shape[1]), lambda i: (0, 0)),
            pl.BlockSpec(w_a.shape, lambda i: (0, 0), **resident),
            pl.BlockSpec(w_b.shape, lambda i: (0, 0), **resident),
            pl.BlockSpec((tm, d), lambda i: (i, gate_col // d)),
            pl.BlockSpec((tm, d), lambda i: (i, gate_col // d + 1)),
            pl.BlockSpec((2, d), lambda i: (0, 0)),
            pl.BlockSpec(w_o.shape, lambda i: (0, 0), **resident),
            pl.BlockSpec((tm, d), lambda i: (i, 0)),
            pl.BlockSpec((1, d), lambda i: (0, 0)),
        ],
        out_specs=pl.BlockSpec((tm, d), lambda i: (i, 0)),
        out_shape=jax.ShapeDtypeStruct((n, d), F32),
        compiler_params=_params(("parallel",)),
        name="mix_out",
    )(pm, o, z, gla_norm, w_a, w_b, z, z, bias, w_o, x2, g_post)


def _ffn_kernel(x_ref, gpre_ref, wg_ref, wu_ref, wd_ref, gpost_ref, o_ref, h_ref, acc_ref, *, splits):
    j = pl.program_id(1)
    last = pl.num_programs(1) - 1
    rb = x_ref.shape[0] // splits
    blocks = [slice(r0, r0 + rb) for r0 in range(0, x_ref.shape[0], rb)]

    def hidden_tile(h):
        gate = _dot(h, wg_ref[...])
        up = _dot(h, wu_ref[...])
        act = (gate * jax.nn.sigmoid(gate) * up).astype(BF16)
        return _dot(act, wd_ref[...])

    @pl.when(j == 0)
    def _():
        for rows in blocks:
            h = _rms(x_ref[rows, :], gpre_ref[...]).astype(BF16)
            h_ref[rows, :] = h
            acc_ref[rows, :] = hidden_tile(h)

    @pl.when((j > 0) & (j < last))
    def _():
        acc_ref[...] += hidden_tile(h_ref[...])

    @pl.when(j == last)
    def _():
        for rows in blocks:
            f = acc_ref[rows, :] + hidden_tile(h_ref[rows, :])
            o_ref[rows, :] = x_ref[rows, :] + _rms(f, gpost_ref[...])


def _ffn(x1, g_pre, w_gate, w_up, w_down, g_post, tm, tf):
    n, d = x1.shape
    dff = w_gate.shape[1]
    nf = dff // tf
    ft = lambda i, j: jnp.where(i % 2 == 0, j, nf - 1 - j)
    return pl.pallas_call(
        functools.partial(_ffn_kernel, splits=2),
        grid=(n // tm, nf),
        in_specs=[
            pl.BlockSpec((tm, d), lambda i, j: (i, 0)),
            pl.BlockSpec((1, d), lambda i, j: (0, 0)),
            pl.BlockSpec((d, tf), lambda i, j: (0, ft(i, j))),
            pl.BlockSpec((d, tf), lambda i, j: (0, ft(i, j))),
            pl.BlockSpec((tf, d), lambda i, j: (ft(i, j), 0)),
            pl.BlockSpec((1, d), lambda i, j: (0, 0)),
        ],
        out_specs=pl.BlockSpec((tm, d), lambda i, j: (i, 0)),
        out_shape=jax.ShapeDtypeStruct((n, d), F32),
        scratch_shapes=[pltpu.VMEM((tm, d), BF16), pltpu.VMEM((tm, d), F32)],
        compiler_params=_params(("parallel", "arbitrary")),
        name="ffn",
    )(x1, g_pre, w_gate, w_up, w_down, g_post)


def _layer(x2, batch, seq, norm_mix_pre, w_in, w_gate_up, b_gate, w_pool, pool_scale, gla_norm,
           w_branch_a, w_branch_b, b_branch_gates, w_out, norm_mix_post,
           norm_ffn_pre, w_ffn_gate, w_ffn_up, w_ffn_down, norm_ffn_post):
    d = x2.shape[1]
    pool_w = w_branch_a.shape[0]
    key_w = w_gate_up.shape[1]
    val_w = w_branch_b.shape[0]
    rank = w_gate_up.shape[0]

    o_v = pool_w + 2 * key_w
    o_g = o_v + val_w
    o_r = o_g + rank
    tn_lo = 1024
    w_in_b = w_in.astype(BF16)
    w_hi = w_in_b[:, o_r:]
    n_lo = o_g // tn_lo
    n_front = o_v // tn_lo
    c_v, c_p = 0, val_w
    c_q, c_k = c_p + pool_w, c_p + pool_w + key_w
    lo_out_tile = lambda s: jnp.where(s < n_front, s + val_w // tn_lo, s - n_front)
    c_r, c_gate = 0, val_w
    w_glr = jnp.pad(w_in_b[:, o_g:o_r], ((0, 0), (0, LANES - rank)))
    w_gu = jnp.pad(w_gate_up, ((0, LANES - rank), (0, 0))).astype(BF16)

    z_lo, z_hi, glr = _inproj(x2, norm_mix_pre[None, :], w_in_b, w_hi, w_glr, n_lo, lo_out_tile,
                              tm=1024, tn_lo=tn_lo, tn_hi=2048)
    pm = _pool(z_lo, w_pool.astype(BF16), pool_scale[None, :], batch, seq, c_p // pool_w, tt=2048)
    o = _gla(z_lo, glr, w_gu, b_gate[None, :], val_w, batch, seq, (c_q, c_k, c_v), tt=1024)
    x1 = _mix_out(pm, o, z_hi, gla_norm[None, :], w_branch_a.astype(BF16), w_branch_b.astype(BF16),
                  b_branch_gates, w_out.astype(BF16), x2, norm_mix_post[None, :], c_r, c_gate, tm=512)
    return _ffn(x1, norm_ffn_pre[None, :], w_ffn_gate.astype(BF16), w_ffn_up.astype(BF16),
                w_ffn_down.astype(BF16), norm_ffn_post[None, :], tm=1024, tf=512)


def kernel(x, norm_mix_pre, w_in, w_gate_up, b_gate, w_pool, pool_scale, gla_norm, w_branch_a,
           w_branch_b, b_branch_gates, w_out, norm_mix_post, norm_ffn_pre, w_ffn_gate, w_ffn_up,
           w_ffn_down, norm_ffn_post):
    batch, seq, d = x.shape
    x2 = x.reshape(batch * seq, d)
    params = (norm_mix_pre, w_in, w_gate_up, b_gate, w_pool, pool_scale, gla_norm, w_branch_a,
              w_branch_b, b_branch_gates, w_out, norm_mix_post, norm_ffn_pre, w_ffn_gate,
              w_ffn_up, w_ffn_down, norm_ffn_post)
    for layer in range(norm_mix_pre.shape[0]):
        x2 = _layer(x2, batch, seq, *(p[layer] for p in params))
    return x2.reshape(batch, seq, d)
```

```python
import functools

import jax
import jax.numpy as jnp
from jax import lax
from jax.experimental import pallas as pl
from jax.experimental.pallas import tpu as pltpu

F32 = jnp.float32
BF16 = jnp.bfloat16

EPS = 1e-6
POOL_WINDOWS = (2, 4, 8, 16)
POOL_HALO = 16
GLA_HEADS = 4
GLA_GATE_TAU = 16.0
GLA_CHUNK = 128
LOG2E = 1.4426950408889634
SUBLANES = 8
LANES = 128
VMEM_LIMIT = 62 * 1024 * 1024


def _params(sem):
    return pltpu.CompilerParams(dimension_semantics=sem, vmem_limit_bytes=VMEM_LIMIT)


def _rms(x, g):
    ms = jnp.mean(x * x, axis=-1, keepdims=True)
    return x * lax.rsqrt(ms + EPS) * g


def _dot(a, b):
    return jnp.dot(a, b, preferred_element_type=F32)


def _inproj_kernel(x_ref, g_ref, wlo_ref, whi_ref, wglr_ref, zlo_ref, zhi_ref, glr_ref, h_ref,
                   *, n_lo, splits):
    j = pl.program_id(1)

    @pl.when(j == 0)
    def _():
        rb = x_ref.shape[0] // splits
        for r0 in range(0, x_ref.shape[0], rb):
            rows = slice(r0, r0 + rb)
            hb = _rms(x_ref[rows, :], g_ref[...]).astype(BF16)
            h_ref[rows, :] = hb
            glr_ref[rows, :] = _dot(hb, wglr_ref[...])
            zlo_ref[rows, :] = _dot(hb, wlo_ref[...]).astype(BF16)

    @pl.when((j > 0) & (j < n_lo))
    def _():
        zlo_ref[...] = _dot(h_ref[...], wlo_ref[...]).astype(BF16)

    @pl.when(j >= n_lo)
    def _():
        zhi_ref[...] = _dot(h_ref[...], whi_ref[...]).astype(BF16)


def _inproj(x2, g, w_lo, w_hi, w_glr, n_lo, lo_out_tile, tm, tn_lo, tn_hi):
    n, d = x2.shape
    n_hi = w_hi.shape[1] // tn_hi

    def lo_tile(i, j):
        t = jnp.minimum(j, n_lo - 1)
        return jnp.where(i % 2 == 0, t, n_lo - 1 - t)

    def hi_tile(i, j):
        t = jnp.maximum(j - n_lo, 0)
        return jnp.where(i % 2 == 0, t, n_hi - 1 - t)

    return pl.pallas_call(
        functools.partial(_inproj_kernel, n_lo=n_lo, splits=4),
        grid=(n // tm, n_lo + n_hi),
        in_specs=[
            pl.BlockSpec((tm, d), lambda i, j: (i, 0)),
            pl.BlockSpec((1, d), lambda i, j: (0, 0)),
            pl.BlockSpec((d, tn_lo), lambda i, j: (0, lo_tile(i, j))),
            pl.BlockSpec((d, tn_hi), lambda i, j: (0, hi_tile(i, j))),
            pl.BlockSpec((d, LANES), lambda i, j: (0, 0)),
        ],
        out_specs=[
            pl.BlockSpec((tm, tn_lo), lambda i, j: (i, lo_out_tile(lo_tile(i, j)))),
            pl.BlockSpec((tm, tn_hi), lambda i, j: (i, hi_tile(i, j))),
            pl.BlockSpec((tm, LANES), lambda i, j: (i, 0)),
        ],
        out_shape=[
            jax.ShapeDtypeStruct((n, n_lo * tn_lo), BF16),
            jax.ShapeDtypeStruct((n, n_hi * tn_hi), BF16),
            jax.ShapeDtypeStruct((n, LANES), F32),
        ],
        scratch_shapes=[pltpu.VMEM((tm, d), BF16)],
        compiler_params=_params(("parallel", "arbitrary")),
        name="in_proj",
    )(x2, g, w_lo, w_hi, w_glr)


def _pool_kernel(p_ref, wp_ref, sc_ref, o_ref, carry_ref, *, tt):
    t = pl.program_id(1)

    @pl.when(t == 0)
    def _():
        carry_ref[...] = jnp.zeros_like(carry_ref)

    p = p_ref[...].astype(F32)
    ext = jnp.concatenate([carry_ref[...], p], axis=0)
    carry_ref[...] = p[tt - POOL_HALO:, :]
    pos = t * tt + lax.broadcasted_iota(jnp.int32, (tt, 1), 0)
    gd = wp_ref.shape[1]
    outs = []
    for g, w in enumerate(POOL_WINDOWS):
        s = ext[:, g * gd:(g + 1) * gd]
        shift = 1
        while shift < w:
            s = s + pltpu.roll(s, shift, 0)
            shift *= 2
        inv_count = 1.0 / jnp.minimum(pos + 1, w).astype(F32)
        d = s[POOL_HALO:, :] * inv_count - p[:, g * gd:(g + 1) * gd]
        y = _dot(d.astype(BF16), wp_ref[g])
        outs.append(y * sc_ref[:, g * gd:(g + 1) * gd])
    o_ref[...] = jnp.concatenate(outs, axis=1).astype(BF16)


def _pool(z, w_pool, scale, batch, seq, col_block, tt):
    width = w_pool.shape[0] * w_pool.shape[1]
    nt = seq // tt
    return pl.pallas_call(
        functools.partial(_pool_kernel, tt=tt),
        grid=(batch, nt),
        in_specs=[
            pl.BlockSpec((tt, width), lambda b, t: (b * nt + t, col_block)),
            pl.BlockSpec(w_pool.shape, lambda b, t: (0, 0, 0)),
            pl.BlockSpec((1, width), lambda b, t: (0, 0)),
        ],
        out_specs=pl.BlockSpec((tt, width), lambda b, t: (b * nt + t, 0)),
        out_shape=jax.ShapeDtypeStruct((batch * seq, width), BF16),
        scratch_shapes=[pltpu.VMEM((POOL_HALO, width), F32)],
        compiler_params=_params(("parallel", "arbitrary")),
        name="pool_mixer",
    )(z, w_pool, scale)


_NT = (((1,), (1,)), ((), ()))
_TN = (((0,), (0,)), ((), ()))


def _store_tiles(ref, x):
    for j in range(ref.shape[0]):
        ref[j] = x[:, j * LANES:(j + 1) * LANES]


def _rows(ref, r0, n):
    return jnp.concatenate([ref[j, pl.ds(r0, n), :] for j in range(ref.shape[0])], axis=1)


def _bcast_row(ref, r, n):
    return jnp.concatenate([ref[j, pl.ds(r, n, stride=0), :] for j in range(ref.shape[0])], axis=1)


def _gla_kernel(q_ref, k_ref, v_ref, glr_ref, wgu_ref, bg_ref,
                o_ref, s_ref, g_scr, q_scr, k_scr, *, tt, scale):
    c = GLA_CHUNK
    kw = q_ref.shape[1]
    dk = kw // GLA_HEADS
    dv = v_ref.shape[1] // GLA_HEADS
    heads = range(GLA_HEADS)

    @pl.when(pl.program_id(1) == 0)
    def _():
        s_ref[...] = jnp.zeros_like(s_ref)

    row = lax.broadcasted_iota(jnp.int32, (c, c), 0)
    col = lax.broadcasted_iota(jnp.int32, (c, c), 1)
    tri = (row >= col).astype(BF16)
    sel_r = lax.broadcasted_iota(jnp.int32, (LANES, c), 0)
    sel_c = lax.broadcasted_iota(jnp.int32, (LANES, c), 1)
    sel = (sel_r == (sel_c % SUBLANES)).astype(BF16)
    lane = lax.broadcasted_iota(jnp.int32, (SUBLANES, LANES), 1)
    sub = lax.broadcasted_iota(jnp.int32, (SUBLANES, LANES), 0)
    slot_mask = [(lane == s) & (sub >= s) for s in range(SUBLANES)]
    in_group = ((row // SUBLANES) == (col // SUBLANES)) & (row >= col)
    level_sizes = []
    b = c // 2
    while b >= 2 * SUBLANES:
        level_sizes.append(b)
        b //= 2
    same_block = [(row // b) == (col // b) for b in level_sizes]

    for ci in range(tt // c):
        rows = slice(ci * c, (ci + 1) * c)
        u = _dot(glr_ref[rows, :].astype(BF16), wgu_ref[...]) + bg_ref[...]
        la = (jnp.minimum(u, 0.0) - jnp.log(1.0 + jnp.exp(-jnp.abs(u)))) * (LOG2E / GLA_GATE_TAU)
        la_hi = la.astype(BF16)
        la_lo = (la - la_hi.astype(F32)).astype(BF16)
        gcum = _dot(tri, la_hi) + _dot(tri, la_lo)
        _store_tiles(g_scr, gcum)
        _store_tiles(q_scr, q_ref[rows, :].astype(F32) * scale)
        _store_tiles(k_scr, k_ref[rows, :].astype(F32))

        def level(b):
            half = b // 2
            qs, ks = [], []
            for s in range(0, c, b):
                gref = _bcast_row(g_scr, s + half, half)
                ks.append((_rows(k_scr, s, half) * jnp.exp2(gref - _rows(g_scr, s, half))).astype(BF16))
                ks.append(jnp.zeros((half, kw), BF16))
                qs.append(jnp.zeros((half, kw), BF16))
                qs.append((_rows(q_scr, s + half, half)
                           * jnp.exp2(_rows(g_scr, s + half, half) - gref)).astype(BF16))
            qb, kb = jnp.concatenate(qs, 0), jnp.concatenate(ks, 0)
            return [lax.dot_general(qb[:, h * dk:(h + 1) * dk], kb[:, h * dk:(h + 1) * dk], _NT,
                                    preferred_element_type=F32) for h in heads]

        a = level(c)
        for b, mask in zip(level_sizes, same_block):
            a = [jnp.where(mask, new, old) for new, old in zip(level(b), a)]

        groups = [[] for _ in heads]
        for g0 in range(0, c, SUBLANES):
            qg, gg = _rows(q_scr, g0, SUBLANES), _rows(g_scr, g0, SUBLANES)
            slots = [jnp.zeros((SUBLANES, LANES), F32) for _ in heads]
            for s in range(SUBLANES):
                p = qg * _bcast_row(k_scr, g0 + s, SUBLANES) * jnp.exp2(gg - _bcast_row(g_scr, g0 + s, SUBLANES))
                for h in heads:
                    score = jnp.sum(p[:, h * dk:(h + 1) * dk], axis=1, keepdims=True)
                    slots[h] = jnp.where(slot_mask[s], score, slots[h])
            for h in heads:
                groups[h].append(slots[h])
        a = [jnp.where(in_group, _dot(jnp.concatenate(groups[h], 0).astype(BF16), sel), a[h]).astype(BF16)
             for h in heads]

        gcum = _rows(g_scr, 0, c)
        g_last8 = _bcast_row(g_scr, c - 1, SUBLANES)
        g_last = jnp.concatenate([g_last8] * (c // SUBLANES), 0)
        qg = (_rows(q_scr, 0, c) * jnp.exp2(gcum)).astype(BF16)
        kd = (_rows(k_scr, 0, c) * jnp.exp2(g_last - gcum)).astype(BF16)
        decay_t = jnp.concatenate([jnp.exp2(g_last8)] * (LANES // SUBLANES), 0)
        for h in heads:
            kc = slice(h * dk, (h + 1) * dk)
            vc = slice(h * dv, (h + 1) * dv)
            v = v_ref[rows, vc]
            s_old = s_ref[h]
            o = _dot(qg[:, kc], s_old.astype(BF16)) + _dot(a[h], v)
            decay = decay_t[:, kc].T
            decay = jnp.concatenate([decay] * (dv // LANES), axis=1)
            s_ref[h] = decay * s_old + lax.dot_general(kd[:, kc], v, _TN, preferred_element_type=F32)
            o_ref[rows, vc] = o.astype(BF16)


def _gla(z, glr, w_gate_up, b_gate, val_w, batch, seq, cols, tt):
    key_w = w_gate_up.shape[1]
    dk = key_w // GLA_HEADS
    dv = val_w // GLA_HEADS
    nt = seq // tt
    q0, k0, v0 = cols
    tiles = key_w // LANES
    return pl.pallas_call(
        functools.partial(_gla_kernel, tt=tt, scale=dk ** -0.5),
        grid=(batch, nt),
        in_specs=[
            pl.BlockSpec((tt, key_w), lambda b, t: (b * nt + t, q0 // key_w)),
            pl.BlockSpec((tt, key_w), lambda b, t: (b * nt + t, k0 // key_w)),
            pl.BlockSpec((tt, val_w), lambda b, t: (b * nt + t, v0 // val_w)),
            pl.BlockSpec((tt, LANES), lambda b, t: (b * nt + t, 0)),
            pl.BlockSpec((LANES, key_w), lambda b, t: (0, 0)),
            pl.BlockSpec((1, key_w), lambda b, t: (0, 0)),
        ],
        out_specs=pl.BlockSpec((tt, val_w), lambda b, t: (b * nt + t, 0)),
        out_shape=jax.ShapeDtypeStruct((batch * seq, val_w), BF16),
        scratch_shapes=[
            pltpu.VMEM((GLA_HEADS, dk, dv), F32),
            pltpu.VMEM((tiles, GLA_CHUNK, LANES), F32),
            pltpu.VMEM((tiles, GLA_CHUNK, LANES), F32),
            pltpu.VMEM((tiles, GLA_CHUNK, LANES), F32),
        ],
        compiler_params=_params(("parallel", "arbitrary")),
        name="gla",
    )(z, z, z, glr, w_gate_up, b_gate)


def _mix_out_kernel(pm_ref, o_ref, r_ref, gn_ref, wa_ref, wb_ref, ga_ref, gb_ref, bias_ref,
                    wo_ref, x_ref, gpost_ref, out_ref):
    dv = gn_ref.shape[1]
    y_b = None
    for h in range(o_ref.shape[1] // dv):
        vc = slice(h * dv, (h + 1) * dv)
        r = r_ref[:, vc].astype(F32)
        on = (_rms(o_ref[:, vc].astype(F32), gn_ref[...]) * (r * jax.nn.sigmoid(r))).astype(BF16)
        part = _dot(on, wb_ref[vc, :])
        y_b = part if y_b is None else y_b + part
    y_a = _dot(pm_ref[...], wa_ref[...])
    gate_a = jax.nn.sigmoid(ga_ref[...].astype(F32) + bias_ref[0:1, :])
    gate_b = jax.nn.sigmoid(gb_ref[...].astype(F32) + bias_ref[1:2, :])
    mixed = (gate_a * y_a + gate_b * y_b).astype(BF16)
    out_ref[...] = x_ref[...] + _rms(_dot(mixed, wo_ref[...]), gpost_ref[...])


def _mix_out(pm, o, z, gla_norm, w_a, w_b, bias, w_o, x2, g_post, r_col, gate_col, tm):
    n, d = x2.shape
    val_w = o.shape[1]
    resident = dict(pipeline_mode=pl.Buffered(1))
    return pl.pallas_call(
        _mix_out_kernel,
        grid=(n // tm,),
        in_specs=[
            pl.BlockSpec((tm, pm.shape[1]), lambda i: (i, 0)),
            pl.BlockSpec((tm, val_w), lambda i: (i, 0)),
            pl.BlockSpec((tm, val_w), lambda i: (i, r_col // val_w)),
            pl.BlockSpec((1, gla_norm.shape[1]), lambda i: (0, 0)),
            pl.BlockSpec(w_a.shape, lambda i: (0, 0), **resident),
            pl.BlockSpec(w_b.shape, lambda i: (0, 0), **resident),
            pl.BlockSpec((tm, d), lambda i: (i, gate_col // d)),
            pl.BlockSpec((tm, d), lambda i: (i, gate_col // d + 1)),
            pl.BlockSpec((2, d), lambda i: (0, 0)),
            pl.BlockSpec(w_o.shape, lambda i: (0, 0), **resident),
            pl.BlockSpec((tm, d), lambda i: (i, 0)),
            pl.BlockSpec((1, d), lambda i: (0, 0)),
        ],
        out_specs=pl.BlockSpec((tm, d), lambda i: (i, 0)),
        out_shape=jax.ShapeDtypeStruct((n, d), F32),
        compiler_params=_params(("parallel",)),
        name="mix_out",
    )(pm, o, z, gla_norm, w_a, w_b, z, z, bias, w_o, x2, g_post)


def _ffn_kernel(x_ref, gpre_ref, wg_ref, wu_ref, wd_ref, gpost_ref, o_ref, h_ref, acc_ref, *, splits):
    j = pl.program_id(1)
    last = pl.num_programs(1) - 1
    rb = x_ref.shape[0] // splits
    blocks = [slice(r0, r0 + rb) for r0 in range(0, x_ref.shape[0], rb)]

    def hidden_tile(h):
        gate = _dot(h, wg_ref[...])
        up = _dot(h, wu_ref[...])
        act = (gate * jax.nn.sigmoid(gate) * up).astype(BF16)
        return _dot(act, wd_ref[...])

    @pl.when(j == 0)
    def _():
        for rows in blocks:
            h = _rms(x_ref[rows, :], gpre_ref[...]).astype(BF16)
            h_ref[rows, :] = h
            acc_ref[rows, :] = hidden_tile(h)

    @pl.when((j > 0) & (j < last))
    def _():
        acc_ref[...] += hidden_tile(h_ref[...])

    @pl.when(j == last)
    def _():
        for rows in blocks:
            f = acc_ref[rows, :] + hidden_tile(h_ref[rows, :])
            o_ref[rows, :] = x_ref[rows, :] + _rms(f, gpost_ref[...])


def _ffn(x1, g_pre, w_gate, w_up, w_down, g_post, tm, tf):
    n, d = x1.shape
    dff = w_gate.shape[1]
    nf = dff // tf
    ft = lambda i, j: jnp.where(i % 2 == 0, j, nf - 1 - j)
    return pl.pallas_call(
        functools.partial(_ffn_kernel, splits=2),
        grid=(n // tm, nf),
        in_specs=[
            pl.BlockSpec((tm, d), lambda i, j: (i, 0)),
            pl.BlockSpec((1, d), lambda i, j: (0, 0)),
            pl.BlockSpec((d, tf), lambda i, j: (0, ft(i, j))),
            pl.BlockSpec((d, tf), lambda i, j: (0, ft(i, j))),
            pl.BlockSpec((tf, d), lambda i, j: (ft(i, j), 0)),
            pl.BlockSpec((1, d), lambda i, j: (0, 0)),
        ],
        out_specs=pl.BlockSpec((tm, d), lambda i, j: (i, 0)),
        out_shape=jax.ShapeDtypeStruct((n, d), F32),
        scratch_shapes=[pltpu.VMEM((tm, d), BF16), pltpu.VMEM((tm, d), F32)],
        compiler_params=_params(("parallel", "arbitrary")),
        name="ffn",
    )(x1, g_pre, w_gate, w_up, w_down, g_post)


def _layer(x2, batch, seq, norm_mix_pre, w_in, w_gate_up, b_gate, w_pool, pool_scale, gla_norm,
           w_branch_a, w_branch_b, b_branch_gates, w_out, norm_mix_post,
           norm_ffn_pre, w_ffn_gate, w_ffn_up, w_ffn_down, norm_ffn_post):
    pool_w = w_branch_a.shape[0]
    key_w = w_gate_up.shape[1]
    val_w = w_branch_b.shape[0]
    rank = w_gate_up.shape[0]

    o_v = pool_w + 2 * key_w
    o_g = o_v + val_w
    o_r = o_g + rank
    tn_lo = 1024
    w_in_b = w_in.astype(BF16)
    w_hi = w_in_b[:, o_r:]
    n_lo = o_g // tn_lo
    n_front = o_v // tn_lo
    c_v, c_p = 0, val_w
    c_q, c_k = c_p + pool_w, c_p + pool_w + key_w
    lo_out_tile = lambda s: jnp.where(s < n_front, s + val_w // tn_lo, s - n_front)
    c_r, c_gate = 0, val_w
    w_glr = jnp.pad(w_in_b[:, o_g:o_r], ((0, 0), (0, LANES - rank)))
    w_gu = jnp.pad(w_gate_up, ((0, LANES - rank), (0, 0))).astype(BF16)

    z_lo, z_hi, glr = _inproj(x2, norm_mix_pre[None, :], w_in_b, w_hi, w_glr, n_lo, lo_out_tile,
                              tm=1024, tn_lo=tn_lo, tn_hi=2048)
    pm = _pool(z_lo, w_pool.astype(BF16), pool_scale[None, :], batch, seq, c_p // pool_w, tt=2048)
    o = _gla(z_lo, glr, w_gu, b_gate[None, :], val_w, batch, seq, (c_q, c_k, c_v), tt=512)
    x1 = _mix_out(pm, o, z_hi, gla_norm[None, :], w_branch_a.astype(BF16), w_branch_b.astype(BF16),
                  b_branch_gates, w_out.astype(BF16), x2, norm_mix_post[None, :], c_r, c_gate, tm=512)
    return _ffn(x1, norm_ffn_pre[None, :], w_ffn_gate.astype(BF16), w_ffn_up.astype(BF16),
                w_ffn_down.astype(BF16), norm_ffn_post[None, :], tm=1024, tf=512)


def kernel(x, norm_mix_pre, w_in, w_gate_up, b_gate, w_pool, pool_scale, gla_norm, w_branch_a,
           w_branch_b, b_branch_gates, w_out, norm_mix_post, norm_ffn_pre, w_ffn_gate, w_ffn_up,
           w_ffn_down, norm_ffn_post):
    batch, seq, d = x.shape
    x2 = x.reshape(batch * seq, d)
    params = (norm_mix_pre, w_in, w_gate_up, b_gate, w_pool, pool_scale, gla_norm, w_branch_a,
              w_branch_b, b_branch_gates, w_out, norm_mix_post, norm_ffn_pre, w_ffn_gate,
              w_ffn_up, w_ffn_down, norm_ffn_post)
    for layer in range(norm_mix_pre.shape[0]):
        x2 = _layer(x2, batch, seq, *(p[layer] for p in params))
    return x2.reshape(batch, seq, d)
```

```python
import functools

import jax
import jax.numpy as jnp
from jax import lax
from jax.experimental import pallas as pl
from jax.experimental.pallas import tpu as pltpu

F32 = jnp.float32
BF16 = jnp.bfloat16

EPS = 1e-6
POOL_WINDOWS = (2, 4, 8, 16)
POOL_HALO = 16
GLA_HEADS = 4
GLA_GATE_TAU = 16.0
GLA_CHUNK = 128
LOG2E = 1.4426950408889634
SUBLANES = 8
LANES = 128
VMEM_LIMIT = 62 * 1024 * 1024


def _params(sem):
    return pltpu.CompilerParams(dimension_semantics=sem, vmem_limit_bytes=VMEM_LIMIT)


def _rms(x, g):
    ms = jnp.mean(x * x, axis=-1, keepdims=True)
    return x * lax.rsqrt(ms + EPS) * g


def _dot(a, b):
    return jnp.dot(a, b, preferred_element_type=F32)


def _inproj_kernel(x_ref, g_ref, wlo_ref, whi_ref, wglr_ref, zlo_ref, zhi_ref, glr_ref, h_ref,
                   *, n_lo, splits):
    j = pl.program_id(1)

    @pl.when(j == 0)
    def _():
        rb = x_ref.shape[0] // splits
        for r0 in range(0, x_ref.shape[0], rb):
            rows = slice(r0, r0 + rb)
            hb = _rms(x_ref[rows, :], g_ref[...]).astype(BF16)
            h_ref[rows, :] = hb
            glr_ref[rows, :] = _dot(hb, wglr_ref[...])
            zlo_ref[rows, :] = _dot(hb, wlo_ref[...]).astype(BF16)

    @pl.when((j > 0) & (j < n_lo))
    def _():
        zlo_ref[...] = _dot(h_ref[...], wlo_ref[...]).astype(BF16)

    @pl.when(j >= n_lo)
    def _():
        zhi_ref[...] = _dot(h_ref[...], whi_ref[...]).astype(BF16)


def _inproj(x2, g, w_lo, w_hi, w_glr, n_lo, lo_out_tile, tm, tn_lo, tn_hi):
    n, d = x2.shape
    n_hi = w_hi.shape[1] // tn_hi

    def lo_tile(i, j):
        t = jnp.minimum(j, n_lo - 1)
        return jnp.where(i % 2 == 0, t, n_lo - 1 - t)

    def hi_tile(i, j):
        t = jnp.maximum(j - n_lo, 0)
        return jnp.where(i % 2 == 0, t, n_hi - 1 - t)

    return pl.pallas_call(
        functools.partial(_inproj_kernel, n_lo=n_lo, splits=4),
        grid=(n // tm, n_lo + n_hi),
        in_specs=[
            pl.BlockSpec((tm, d), lambda i, j: (i, 0)),
            pl.BlockSpec((1, d), lambda i, j: (0, 0)),
            pl.BlockSpec((d, tn_lo), lambda i, j: (0, lo_tile(i, j))),
            pl.BlockSpec((d, tn_hi), lambda i, j: (0, hi_tile(i, j))),
            pl.BlockSpec((d, LANES), lambda i, j: (0, 0)),
        ],
        out_specs=[
            pl.BlockSpec((tm, tn_lo), lambda i, j: (i, lo_out_tile(lo_tile(i, j)))),
            pl.BlockSpec((tm, tn_hi), lambda i, j: (i, hi_tile(i, j))),
            pl.BlockSpec((tm, LANES), lambda i, j: (i, 0)),
        ],
        out_shape=[
            jax.ShapeDtypeStruct((n, n_lo * tn_lo), BF16),
            jax.ShapeDtypeStruct((n, n_hi * tn_hi), BF16),
            jax.ShapeDtypeStruct((n, LANES), F32),
        ],
        scratch_shapes=[pltpu.VMEM((tm, d), BF16)],
        compiler_params=_params(("parallel", "arbitrary")),
        name="in_proj",
    )(x2, g, w_lo, w_hi, w_glr)


def _pool_kernel(p_ref, wp_ref, sc_ref, o_ref, carry_ref, *, tt):
    t = pl.program_id(1)

    @pl.when(t == 0)
    def _():
        carry_ref[...] = jnp.zeros_like(carry_ref)

    p = p_ref[...].astype(F32)
    ext = jnp.concatenate([carry_ref[...], p], axis=0)
    carry_ref[...] = p[tt - POOL_HALO:, :]
    pos = t * tt + lax.broadcasted_iota(jnp.int32, (tt, 1), 0)
    gd = wp_ref.shape[1]
    outs = []
    for g, w in enumerate(POOL_WINDOWS):
        s = ext[:, g * gd:(g + 1) * gd]
        shift = 1
        while shift < w:
            s = s + pltpu.roll(s, shift, 0)
            shift *= 2
        inv_count = 1.0 / jnp.minimum(pos + 1, w).astype(F32)
        d = s[POOL_HALO:, :] * inv_count - p[:, g * gd:(g + 1) * gd]
        y = _dot(d.astype(BF16), wp_ref[g])
        outs.append(y * sc_ref[:, g * gd:(g + 1) * gd])
    o_ref[...] = jnp.concatenate(outs, axis=1).astype(BF16)


def _pool(z, w_pool, scale, batch, seq, col_block, tt):
    width = w_pool.shape[0] * w_pool.shape[1]
    nt = seq // tt
    return pl.pallas_call(
        functools.partial(_pool_kernel, tt=tt),
        grid=(batch, nt),
        in_specs=[
            pl.BlockSpec((tt, width), lambda b, t: (b * nt + t, col_block)),
            pl.BlockSpec(w_pool.shape, lambda b, t: (0, 0, 0)),
            pl.BlockSpec((1, width), lambda b, t: (0, 0)),
        ],
        out_specs=pl.BlockSpec((tt, width), lambda b, t: (b * nt + t, 0)),
        out_shape=jax.ShapeDtypeStruct((batch * seq, width), BF16),
        scratch_shapes=[pltpu.VMEM((POOL_HALO, width), F32)],
        compiler_params=_params(("parallel", "arbitrary")),
        name="pool_mixer",
    )(z, w_pool, scale)


_NT = (((1,), (1,)), ((), ()))
_TN = (((0,), (0,)), ((), ()))


def _store_tiles(ref, x):
    for j in range(ref.shape[0]):
        ref[j] = x[:, j * LANES:(j + 1) * LANES]


def _rows(ref, r0, n):
    return jnp.concatenate([ref[j, pl.ds(r0, n), :] for j in range(ref.shape[0])], axis=1)


def _bcast_row(ref, r, n):
    return jnp.concatenate([ref[j, pl.ds(r, n, stride=0), :] for j in range(ref.shape[0])], axis=1)


def _gla_kernel(q_ref, k_ref, v_ref, glr_ref, wgu_ref, bg_ref,
                o_ref, s_ref, g_scr, q_scr, k_scr, *, tt, scale):
    c = GLA_CHUNK
    kw = q_ref.shape[1]
    dk = kw // GLA_HEADS
    dv = v_ref.shape[1] // GLA_HEADS
    heads = range(GLA_HEADS)

    @pl.when(pl.program_id(1) == 0)
    def _():
        s_ref[...] = jnp.zeros_like(s_ref)

    row = lax.broadcasted_iota(jnp.int32, (c, c), 0)
    col = lax.broadcasted_iota(jnp.int32, (c, c), 1)
    tri = (row >= col).astype(BF16)
    sel_r = lax.broadcasted_iota(jnp.int32, (LANES, c), 0)
    sel_c = lax.broadcasted_iota(jnp.int32, (LANES, c), 1)
    sel = (sel_r == (sel_c % SUBLANES)).astype(BF16)
    lane = lax.broadcasted_iota(jnp.int32, (SUBLANES, LANES), 1)
    sub = lax.broadcasted_iota(jnp.int32, (SUBLANES, LANES), 0)
    slot_mask = [(lane == s) & (sub >= s) for s in range(SUBLANES)]
    in_group = ((row // SUBLANES) == (col // SUBLANES)) & (row >= col)
    level_sizes = []
    b = c // 2
    while b >= 2 * SUBLANES:
        level_sizes.append(b)
        b //= 2
    same_block = [(row // b) == (col // b) for b in level_sizes]

    for ci in range(tt // c):
        rows = slice(ci * c, (ci + 1) * c)
        u = _dot(glr_ref[rows, :].astype(BF16), wgu_ref[...]) + bg_ref[...]
        la = (jnp.minimum(u, 0.0) - jnp.log(1.0 + jnp.exp(-jnp.abs(u)))) * (LOG2E / GLA_GATE_TAU)
        la_hi = la.astype(BF16)
        la_lo = (la - la_hi.astype(F32)).astype(BF16)
        gcum = _dot(tri, la_hi) + _dot(tri, la_lo)
        _store_tiles(g_scr, gcum)
        _store_tiles(q_scr, q_ref[rows, :].astype(F32) * scale)
        _store_tiles(k_scr, k_ref[rows, :].astype(F32))

        def level(b):
            half = b // 2
            qs, ks = [], []
            for s in range(0, c, b):
                gref = _bcast_row(g_scr, s + half, half)
                ks.append((_rows(k_scr, s, half) * jnp.exp2(gref - _rows(g_scr, s, half))).astype(BF16))
                ks.append(jnp.zeros((half, kw), BF16))
                qs.append(jnp.zeros((half, kw), BF16))
                qs.append((_rows(q_scr, s + half, half)
                           * jnp.exp2(_rows(g_scr, s + half, half) - gref)).astype(BF16))
            qb, kb = jnp.concatenate(qs, 0), jnp.concatenate(ks, 0)
            return [lax.dot_general(qb[:, h * dk:(h + 1) * dk], kb[:, h * dk:(h + 1) * dk], _NT,
                                    preferred_element_type=F32) for h in heads]

        a = level(c)
        for b, mask in zip(level_sizes, same_block):
            a = [jnp.where(mask, new, old) for new, old in zip(level(b), a)]

        groups = [[] for _ in heads]
        for g0 in range(0, c, SUBLANES):
            qg, gg = _rows(q_scr, g0, SUBLANES), _rows(g_scr, g0, SUBLANES)
            slots = [jnp.zeros((SUBLANES, LANES), F32) for _ in heads]
            for s in range(SUBLANES):
                p = qg * _bcast_row(k_scr, g0 + s, SUBLANES) * jnp.exp2(gg - _bcast_row(g_scr, g0 + s, SUBLANES))
                for h in heads:
                    score = jnp.sum(p[:, h * dk:(h + 1) * dk], axis=1, keepdims=True)
                    slots[h] = jnp.where(slot_mask[s], score, slots[h])
            for h in heads:
                groups[h].append(slots[h])
        a = [jnp.where(in_group, _dot(jnp.concatenate(groups[h], 0).astype(BF16), sel), a[h]).astype(BF16)
             for h in heads]

        gcum = _rows(g_scr, 0, c)
        g_last8 = _bcast_row(g_scr, c - 1, SUBLANES)
        g_last = jnp.concatenate([g_last8] * (c // SUBLANES), 0)
        qg = (_rows(q_scr, 0, c) * jnp.exp2(gcum)).astype(BF16)
        kd = (_rows(k_scr, 0, c) * jnp.exp2(g_last - gcum)).astype(BF16)
        decay_t = jnp.concatenate([jnp.exp2(g_last8)] * (LANES // SUBLANES), 0)
        for h in heads:
            kc = slice(h * dk, (h + 1) * dk)
            vc = slice(h * dv, (h + 1) * dv)
            v = v_ref[rows, vc]
            s_old = s_ref[h]
            o = _dot(qg[:, kc], s_old.astype(BF16)) + _dot(a[h], v)
            decay = decay_t[:, kc].T
            decay = jnp.concatenate([decay] * (dv // LANES), axis=1)
            s_ref[h] = decay * s_old + lax.dot_general(kd[:, kc], v, _TN, preferred_element_type=F32)
            o_ref[rows, vc] = o.astype(BF16)


def _gla(z, glr, w_gate_up, b_gate, val_w, batch, seq, cols, tt):
    key_w = w_gate_up.shape[1]
    dk = key_w // GLA_HEADS
    dv = val_w // GLA_HEADS
    nt = seq // tt
    q0, k0, v0 = cols
    tiles = key_w // LANES
    return pl.pallas_call(
        functools.partial(_gla_kernel, tt=tt, scale=dk ** -0.5),
        grid=(batch, nt),
        in_specs=[
            pl.BlockSpec((tt, key_w), lambda b, t: (b * nt + t, q0 // key_w)),
            pl.BlockSpec((tt, key_w), lambda b, t: (b * nt + t, k0 // key_w)),
            pl.BlockSpec((tt, val_w), lambda b, t: (b * nt + t, v0 // val_w)),
            pl.BlockSpec((tt, LANES), lambda b, t: (b * nt + t, 0)),
            pl.BlockSpec((LANES, key_w), lambda b, t: (0, 0)),
            pl.BlockSpec((1, key_w), lambda b, t: (0, 0)),
        ],
        out_specs=pl.BlockSpec((tt, val_w), lambda b, t: (b * nt + t, 0)),
        out_shape=jax.ShapeDtypeStruct((batch * seq, val_w), BF16),
        scratch_shapes=[
            pltpu.VMEM((GLA_HEADS, dk, dv), F32),
            pltpu.VMEM((tiles, GLA_CHUNK, LANES), F32),
            pltpu.VMEM((tiles, GLA_CHUNK, LANES), F32),
            pltpu.VMEM((tiles, GLA_CHUNK, LANES), F32),
        ],
        compiler_params=_params(("parallel", "arbitrary")),
        name="gla",
    )(z, z, z, glr, w_gate_up, b_gate)


def _mix_out_kernel(pm_ref, o_ref, r_ref, gn_ref, wa_ref, wb_ref, ga_ref, gb_ref, bias_ref,
                    wo_ref, x_ref, gpost_ref, out_ref):
    dv = gn_ref.shape[1]
    y_b = None
    for h in range(o_ref.shape[1] // dv):
        vc = slice(h * dv, (h + 1) * dv)
        r = r_ref[:, vc].astype(F32)
        on = (_rms(o_ref[:, vc].astype(F32), gn_ref[...]) * (r * jax.nn.sigmoid(r))).astype(BF16)
        part = _dot(on, wb_ref[vc, :])
        y_b = part if y_b is None else y_b + part
    y_a = _dot(pm_ref[...], wa_ref[...])
    gate_a = jax.nn.sigmoid(ga_ref[...].astype(F32) + bias_ref[0:1, :])
    gate_b = jax.nn.sigmoid(gb_ref[...].astype(F32) + bias_ref[1:2, :])
    mixed = (gate_a * y_a + gate_b * y_b).astype(BF16)
    out_ref[...] = x_ref[...] + _rms(_dot(mixed, wo_ref[...]), gpost_ref[...])


def _mix_out(pm, o, z, gla_norm, w_a, w_b, bias, w_o, x2, g_post, r_col, gate_col, tm):
    n, d = x2.shape
    val_w = o.shape[1]
    resident = dict(pipeline_mode=pl.Buffered(1))
    return pl.pallas_call(
        _mix_out_kernel,
        grid=(n // tm,),
        in_specs=[
            pl.BlockSpec((tm, pm.shape[1]), lambda i: (i, 0)),
            pl.BlockSpec((tm, val_w), lambda i: (i, 0)),
            pl.BlockSpec((tm, val_w), lambda i: (i, r_col // val_w)),
            pl.BlockSpec((1, gla_norm.shape[1]), lambda i: (0, 0)),
            pl.BlockSpec(w_a.shape, lambda i: (0, 0), **resident),
            pl.BlockSpec(w_b.shape, lambda i: (0, 0), **resident),
            pl.BlockSpec((tm, d), lambda i: (i, gate_col // d)),
            pl.BlockSpec((tm, d), lambda i: (i, gate_col // d + 1)),
            pl.BlockSpec((2, d), lambda i: (0, 0)),
            pl.BlockSpec(w_o.shape, lambda i: (0, 0), **resident),
            pl.BlockSpec((tm, d), lambda i: (i, 0)),
            pl.BlockSpec((1, d), lambda i: (0, 0)),
        ],
        out_specs=pl.BlockSpec((tm, d), lambda i: (i, 0)),
        out_shape=jax.ShapeDtypeStruct((n, d), F32),
        compiler_params=_params(("parallel",)),
        name="mix_out",
    )(pm, o, z, gla_norm, w_a, w_b, z, z, bias, w_o, x2, g_post)


def _ffn_kernel(x_ref, gpre_ref, wg_hbm, wu_hbm, wd_hbm, gpost_ref, o_ref, h_ref, acc_ref, *, tf):
    d = x_ref.shape[1]
    h_ref[...] = _rms(x_ref[...], gpre_ref[...]).astype(BF16)
    acc_ref[...] = jnp.zeros_like(acc_ref)

    def hidden_tile(wg_ref, wu_ref, wd_ref):
        h = h_ref[...]
        gate = _dot(h, wg_ref[...])
        up = _dot(h, wu_ref[...])
        act = (gate * jax.nn.sigmoid(gate) * up).astype(BF16)
        acc_ref[...] += _dot(act, wd_ref[...])

    pltpu.emit_pipeline(
        hidden_tile,
        grid=(wg_hbm.shape[1] // tf,),
        in_specs=[
            pl.BlockSpec((d, tf), lambda j: (0, j)),
            pl.BlockSpec((d, tf), lambda j: (0, j)),
            pl.BlockSpec((tf, d), lambda j: (j, 0)),
        ],
    )(wg_hbm, wu_hbm, wd_hbm)
    o_ref[...] = x_ref[...] + _rms(acc_ref[...], gpost_ref[...])


def _ffn(x1, g_pre, w_gate, w_up, w_down, g_post, tm, tf):
    n, d = x1.shape
    return pl.pallas_call(
        functools.partial(_ffn_kernel, tf=tf),
        grid=(n // tm,),
        in_specs=[
            pl.BlockSpec((tm, d), lambda i: (i, 0)),
            pl.BlockSpec((1, d), lambda i: (0, 0)),
            pl.BlockSpec(memory_space=pl.ANY),
            pl.BlockSpec(memory_space=pl.ANY),
            pl.BlockSpec(memory_space=pl.ANY),
            pl.BlockSpec((1, d), lambda i: (0, 0)),
        ],
        out_specs=pl.BlockSpec((tm, d), lambda i: (i, 0)),
        out_shape=jax.ShapeDtypeStruct((n, d), F32),
        scratch_shapes=[pltpu.VMEM((tm, d), BF16), pltpu.VMEM((tm, d), F32)],
        compiler_params=_params(("parallel",)),
        name="ffn",
    )(x1, g_pre, w_gate, w_up, w_down, g_post)


def _layer(x2, batch, seq, norm_mix_pre, w_in, w_gate_up, b_gate, w_pool, pool_scale, gla_norm,
           w_branch_a, w_branch_b, b_branch_gates, w_out, norm_mix_post,
           norm_ffn_pre, w_ffn_gate, w_ffn_up, w_ffn_down, norm_ffn_post):
    pool_w = w_branch_a.shape[0]
    key_w = w_gate_up.shape[1]
    val_w = w_branch_b.shape[0]
    rank = w_gate_up.shape[0]

    o_v = pool_w + 2 * key_w
    o_g = o_v + val_w
    o_r = o_g + rank
    tn_lo = 1024
    w_in_b = w_in.astype(BF16)
    w_hi = w_in_b[:, o_r:]
    n_lo = o_g // tn_lo
    n_front = o_v // tn_lo
    c_v, c_p = 0, val_w
    c_q, c_k = c_p + pool_w, c_p + pool_w + key_w
    lo_out_tile = lambda s: jnp.where(s < n_front, s + val_w // tn_lo, s - n_front)
    c_r, c_gate = 0, val_w
    w_glr = jnp.pad(w_in_b[:, o_g:o_r], ((0, 0), (0, LANES - rank)))
    w_gu = jnp.pad(w_gate_up, ((0, LANES - rank), (0, 0))).astype(BF16)

    z_lo, z_hi, glr = _inproj(x2, norm_mix_pre[None, :], w_in_b, w_hi, w_glr, n_lo, lo_out_tile,
                              tm=1024, tn_lo=tn_lo, tn_hi=2048)
    pm = _pool(z_lo, w_pool.astype(BF16), pool_scale[None, :], batch, seq, c_p // pool_w, tt=2048)
    o = _gla(z_lo, glr, w_gu, b_gate[None, :], val_w, batch, seq, (c_q, c_k, c_v), tt=512)
    x1 = _mix_out(pm, o, z_hi, gla_norm[None, :], w_branch_a.astype(BF16), w_branch_b.astype(BF16),
                  b_branch_gates, w_out.astype(BF16), x2, norm_mix_post[None, :], c_r, c_gate, tm=512)
    return _ffn(x1, norm_ffn_pre[None, :], w_ffn_gate.astype(BF16), w_ffn_up.astype(BF16),
                w_ffn_down.astype(BF16), norm_ffn_post[None, :], tm=512, tf=512)


def kernel(x, norm_mix_pre, w_in, w_gate_up, b_gate, w_pool, pool_scale, gla_norm, w_branch_a,
           w_branch_b, b_branch_gates, w_out, norm_mix_post, norm_ffn_pre, w_ffn_gate, w_ffn_up,
           w_ffn_down, norm_ffn_post):
    batch, seq, d = x.shape
    x2 = x.reshape(batch * seq, d)
    params = (norm_mix_pre, w_in, w_gate_up, b_gate, w_pool, pool_scale, gla_norm, w_branch_a,
              w_branch_b, b_branch_gates, w_out, norm_mix_post, norm_ffn_pre, w_ffn_gate,
              w_ffn_up, w_ffn_down, norm_ffn_post)
    for layer in range(norm_mix_pre.shape[0]):
        x2 = _layer(x2, batch, seq, *(p[layer] for p in params))
    return x2.reshape(batch, seq, d)
```

```python
import functools

import jax
import jax.numpy as jnp
from jax import lax
from jax.experimental import pallas as pl
from jax.experimental.pallas import tpu as pltpu

F32 = jnp.float32
BF16 = jnp.bfloat16

EPS = 1e-6
POOL_WINDOWS = (2, 4, 8, 16)
POOL_HALO = 16
GLA_HEADS = 4
GLA_GATE_TAU = 16.0
GLA_CHUNK = 128
LOG2E = 1.4426950408889634
SUBLANES = 8
LANES = 128
VMEM_LIMIT = 62 * 1024 * 1024


def _params(sem):
    return pltpu.CompilerParams(dimension_semantics=sem, vmem_limit_bytes=VMEM_LIMIT)


def _rms(x, g):
    ms = jnp.mean(x * x, axis=-1, keepdims=True)
    return x * lax.rsqrt(ms + EPS) * g


def _dot(a, b):
    return jnp.dot(a, b, preferred_element_type=F32)


def _inproj_kernel(x_ref, g_ref, wlo_ref, whi_ref, wglr_ref, zlo_ref, zhi_ref, glr_ref, h_ref,
                   *, n_lo, splits):
    j = pl.program_id(1)

    @pl.when(j == 0)
    def _():
        rb = x_ref.shape[0] // splits
        for r0 in range(0, x_ref.shape[0], rb):
            rows = slice(r0, r0 + rb)
            hb = _rms(x_ref[rows, :], g_ref[...]).astype(BF16)
            h_ref[rows, :] = hb
            glr_ref[rows, :] = _dot(hb, wglr_ref[...])
            zlo_ref[rows, :] = _dot(hb, wlo_ref[...]).astype(BF16)

    @pl.when((j > 0) & (j < n_lo))
    def _():
        zlo_ref[...] = _dot(h_ref[...], wlo_ref[...]).astype(BF16)

    @pl.when(j >= n_lo)
    def _():
        zhi_ref[...] = _dot(h_ref[...], whi_ref[...]).astype(BF16)


def _inproj(x2, g, w_lo, w_hi, w_glr, n_lo, lo_out_tile, tm, tn_lo, tn_hi):
    n, d = x2.shape
    n_hi = w_hi.shape[1] // tn_hi

    def lo_tile(i, j):
        t = jnp.minimum(j, n_lo - 1)
        return jnp.where(i % 2 == 0, t, n_lo - 1 - t)

    def hi_tile(i, j):
        t = jnp.maximum(j - n_lo, 0)
        return jnp.where(i % 2 == 0, t, n_hi - 1 - t)

    return pl.pallas_call(
        functools.partial(_inproj_kernel, n_lo=n_lo, splits=4),
        grid=(n // tm, n_lo + n_hi),
        in_specs=[
            pl.BlockSpec((tm, d), lambda i, j: (i, 0)),
            pl.BlockSpec((1, d), lambda i, j: (0, 0)),
            pl.BlockSpec((d, tn_lo), lambda i, j: (0, lo_tile(i, j))),
            pl.BlockSpec((d, tn_hi), lambda i, j: (0, hi_tile(i, j))),
            pl.BlockSpec((d, LANES), lambda i, j: (0, 0)),
        ],
        out_specs=[
            pl.BlockSpec((tm, tn_lo), lambda i, j: (i, lo_out_tile(lo_tile(i, j)))),
            pl.BlockSpec((tm, tn_hi), lambda i, j: (i, hi_tile(i, j))),
            pl.BlockSpec((tm, LANES), lambda i, j: (i, 0)),
        ],
        out_shape=[
            jax.ShapeDtypeStruct((n, n_lo * tn_lo), BF16),
            jax.ShapeDtypeStruct((n, n_hi * tn_hi), BF16),
            jax.ShapeDtypeStruct((n, LANES), F32),
        ],
        scratch_shapes=[pltpu.VMEM((tm, d), BF16)],
        compiler_params=_params(("parallel", "arbitrary")),
        name="in_proj",
    )(x2, g, w_lo, w_hi, w_glr)


def _pool_kernel(p_ref, wp_ref, sc_ref, o_ref, carry_ref, *, tt):
    t = pl.program_id(1)

    @pl.when(t == 0)
    def _():
        carry_ref[...] = jnp.zeros_like(carry_ref)

    p = p_ref[...].astype(F32)
    ext = jnp.concatenate([carry_ref[...], p], axis=0)
    carry_ref[...] = p[tt - POOL_HALO:, :]
    pos = t * tt + lax.broadcasted_iota(jnp.int32, (tt, 1), 0)
    gd = wp_ref.shape[1]
    outs = []
    for g, w in enumerate(POOL_WINDOWS):
        s = ext[:, g * gd:(g + 1) * gd]
        shift = 1
        while shift < w:
            s = s + pltpu.roll(s, shift, 0)
            shift *= 2
        inv_count = 1.0 / jnp.minimum(pos + 1, w).astype(F32)
        d = s[POOL_HALO:, :] * inv_count - p[:, g * gd:(g + 1) * gd]
        y = _dot(d.astype(BF16), wp_ref[g])
        outs.append(y * sc_ref[:, g * gd:(g + 1) * gd])
    o_ref[...] = jnp.concatenate(outs, axis=1).astype(BF16)


def _pool(z, w_pool, scale, batch, seq, col_block, tt):
    width = w_pool.shape[0] * w_pool.shape[1]
    nt = seq // tt
    return pl.pallas_call(
        functools.partial(_pool_kernel, tt=tt),
        grid=(batch, nt),
        in_specs=[
            pl.BlockSpec((tt, width), lambda b, t: (b * nt + t, col_block)),
            pl.BlockSpec(w_pool.shape, lambda b, t: (0, 0, 0)),
            pl.BlockSpec((1, width), lambda b, t: (0, 0)),
        ],
        out_specs=pl.BlockSpec((tt, width), lambda b, t: (b * nt + t, 0)),
        out_shape=jax.ShapeDtypeStruct((batch * seq, width), BF16),
        scratch_shapes=[pltpu.VMEM((POOL_HALO, width), F32)],
        compiler_params=_params(("parallel", "arbitrary")),
        name="pool_mixer",
    )(z, w_pool, scale)


_NT = (((1,), (1,)), ((), ()))
_TN = (((0,), (0,)), ((), ()))


def _store_tiles(ref, x):
    for j in range(ref.shape[0]):
        ref[j] = x[:, j * LANES:(j + 1) * LANES]


def _rows(ref, r0, n):
    return jnp.concatenate([ref[j, pl.ds(r0, n), :] for j in range(ref.shape[0])], axis=1)


def _bcast_row(ref, r, n):
    return jnp.concatenate([ref[j, pl.ds(r, n, stride=0), :] for j in range(ref.shape[0])], axis=1)


def _gla_kernel(q_ref, k_ref, v_ref, glr_ref, wgu_ref, bg_ref,
                o_ref, s_ref, g_scr, q_scr, k_scr, *, tt, scale):
    c = GLA_CHUNK
    kw = q_ref.shape[1]
    dk = kw // GLA_HEADS
    dv = v_ref.shape[1] // GLA_HEADS
    heads = range(GLA_HEADS)

    @pl.when(pl.program_id(1) == 0)
    def _():
        s_ref[...] = jnp.zeros_like(s_ref)

    row = lax.broadcasted_iota(jnp.int32, (c, c), 0)
    col = lax.broadcasted_iota(jnp.int32, (c, c), 1)
    tri = (row >= col).astype(BF16)
    sel_r = lax.broadcasted_iota(jnp.int32, (LANES, c), 0)
    sel_c = lax.broadcasted_iota(jnp.int32, (LANES, c), 1)
    sel = (sel_r == (sel_c % SUBLANES)).astype(BF16)
    lane = lax.broadcasted_iota(jnp.int32, (SUBLANES, LANES), 1)
    sub = lax.broadcasted_iota(jnp.int32, (SUBLANES, LANES), 0)
    slot_mask = [(lane == s) & (sub >= s) for s in range(SUBLANES)]
    in_group = ((row // SUBLANES) == (col // SUBLANES)) & (row >= col)
    level_sizes = []
    b = c // 2
    while b >= 2 * SUBLANES:
        level_sizes.append(b)
        b //= 2
    same_block = [(row // b) == (col // b) for b in level_sizes]

    for ci in range(tt // c):
        rows = slice(ci * c, (ci + 1) * c)
        u = _dot(glr_ref[rows, :].astype(BF16), wgu_ref[...]) + bg_ref[...]
        la = (jnp.minimum(u, 0.0) - jnp.log(1.0 + jnp.exp(-jnp.abs(u)))) * (LOG2E / GLA_GATE_TAU)
        la_hi = la.astype(BF16)
        la_lo = (la - la_hi.astype(F32)).astype(BF16)
        gcum = _dot(tri, la_hi) + _dot(tri, la_lo)
        _store_tiles(g_scr, gcum)
        _store_tiles(q_scr, q_ref[rows, :].astype(F32) * scale)
        _store_tiles(k_scr, k_ref[rows, :].astype(F32))

        def level(b):
            half = b // 2
            qs, ks = [], []
            for s in range(0, c, b):
                gref = _bcast_row(g_scr, s + half, half)
                ks.append((_rows(k_scr, s, half) * jnp.exp2(gref - _rows(g_scr, s, half))).astype(BF16))
                ks.append(jnp.zeros((half, kw), BF16))
                qs.append(jnp.zeros((half, kw), BF16))
                qs.append((_rows(q_scr, s + half, half)
                           * jnp.exp2(_rows(g_scr, s + half, half) - gref)).astype(BF16))
            qb, kb = jnp.concatenate(qs, 0), jnp.concatenate(ks, 0)
            return [lax.dot_general(qb[:, h * dk:(h + 1) * dk], kb[:, h * dk:(h + 1) * dk], _NT,
                                    preferred_element_type=F32) for h in heads]

        a = level(c)
        for b, mask in zip(level_sizes, same_block):
            a = [jnp.where(mask, new, old) for new, old in zip(level(b), a)]

        groups = [[] for _ in heads]
        for g0 in range(0, c, SUBLANES):
            qg, gg = _rows(q_scr, g0, SUBLANES), _rows(g_scr, g0, SUBLANES)
            slots = [jnp.zeros((SUBLANES, LANES), F32) for _ in heads]
            for s in range(SUBLANES):
                p = qg * _bcast_row(k_scr, g0 + s, SUBLANES) * jnp.exp2(gg - _bcast_row(g_scr, g0 + s, SUBLANES))
                for h in heads:
                    score = jnp.sum(p[:, h * dk:(h + 1) * dk], axis=1, keepdims=True)
                    slots[h] = jnp.where(slot_mask[s], score, slots[h])
            for h in heads:
                groups[h].append(slots[h])
        a = [jnp.where(in_group, _dot(jnp.concatenate(groups[h], 0).astype(BF16), sel), a[h]).astype(BF16)
             for h in heads]

        for h in heads:
            vc = slice(h * dv, (h + 1) * dv)
            th = pl.ds(h * (dk // LANES), dk // LANES)
            gcum = _rows(g_scr.at[th], 0, c)
            g_last8 = _bcast_row(g_scr.at[th], c - 1, SUBLANES)
            g_last = jnp.concatenate([g_last8] * (c // SUBLANES), 0)
            qg = (_rows(q_scr.at[th], 0, c) * jnp.exp2(gcum)).astype(BF16)
            kd = (_rows(k_scr.at[th], 0, c) * jnp.exp2(g_last - gcum)).astype(BF16)
            v = v_ref[rows, vc]
            s_old = s_ref[h]
            o = _dot(qg, s_old.astype(BF16)) + _dot(a[h], v)
            decay = jnp.concatenate([jnp.exp2(g_last8)] * (LANES // SUBLANES), 0).T
            decay = jnp.concatenate([decay] * (dv // LANES), axis=1)
            s_ref[h] = decay * s_old + lax.dot_general(kd, v, _TN, preferred_element_type=F32)
            o_ref[rows, vc] = o.astype(BF16)


def _gla(z, glr, w_gate_up, b_gate, val_w, batch, seq, cols, tt):
    key_w = w_gate_up.shape[1]
    dk = key_w // GLA_HEADS
    dv = val_w // GLA_HEADS
    nt = seq // tt
    q0, k0, v0 = cols
    tiles = key_w // LANES
    return pl.pallas_call(
        functools.partial(_gla_kernel, tt=tt, scale=dk ** -0.5),
        grid=(batch, nt),
        in_specs=[
            pl.BlockSpec((tt, key_w), lambda b, t: (b * nt + t, q0 // key_w)),
            pl.BlockSpec((tt, key_w), lambda b, t: (b * nt + t, k0 // key_w)),
            pl.BlockSpec((tt, val_w), lambda b, t: (b * nt + t, v0 // val_w)),
            pl.BlockSpec((tt, LANES), lambda b, t: (b * nt + t, 0)),
            pl.BlockSpec((LANES, key_w), lambda b, t: (0, 0)),
            pl.BlockSpec((1, key_w), lambda b, t: (0, 0)),
        ],
        out_specs=pl.BlockSpec((tt, val_w), lambda b, t: (b * nt + t, 0)),
        out_shape=jax.ShapeDtypeStruct((batch * seq, val_w), BF16),
        scratch_shapes=[
            pltpu.VMEM((GLA_HEADS, dk, dv), F32),
            pltpu.VMEM((tiles, GLA_CHUNK, LANES), F32),
            pltpu.VMEM((tiles, GLA_CHUNK, LANES), F32),
            pltpu.VMEM((tiles, GLA_CHUNK, LANES), F32),
        ],
        compiler_params=_params(("parallel", "arbitrary")),
        name="gla",
    )(z, z, z, glr, w_gate_up, b_gate)


def _mix_out_kernel(pm_ref, o_ref, r_ref, gn_ref, wa_ref, wb_ref, ga_ref, gb_ref, bias_ref,
                    wo_ref, x_ref, gpost_ref, out_ref):
    dv = gn_ref.shape[1]
    y_b = None
    for h in range(o_ref.shape[1] // dv):
        vc = slice(h * dv, (h + 1) * dv)
        r = r_ref[:, vc].astype(F32)
        on = (_rms(o_ref[:, vc].astype(F32), gn_ref[...]) * (r * jax.nn.sigmoid(r))).astype(BF16)
        part = _dot(on, wb_ref[vc, :])
        y_b = part if y_b is None else y_b + part
    y_a = _dot(pm_ref[...], wa_ref[...])
    gate_a = jax.nn.sigmoid(ga_ref[...].astype(F32) + bias_ref[0:1, :])
    gate_b = jax.nn.sigmoid(gb_ref[...].astype(F32) + bias_ref[1:2, :])
    mixed = (gate_a * y_a + gate_b * y_b).astype(BF16)
    out_ref[...] = x_ref[...] + _rms(_dot(mixed, wo_ref[...]), gpost_ref[...])


def _mix_out(pm, o, z, gla_norm, w_a, w_b, bias, w_o, x2, g_post, r_col, gate_col, tm):
    n, d = x2.shape
    val_w = o.shape[1]
    resident = dict(pipeline_mode=pl.Buffered(1))
    return pl.pallas_call(
        _mix_out_kernel,
        grid=(n // tm,),
        in_specs=[
            pl.BlockSpec((tm, pm.shape[1]), lambda i: (i, 0)),
            pl.BlockSpec((tm, val_w), lambda i: (i, 0)),
            pl.BlockSpec((tm, val_w), lambda i: (i, r_col // val_w)),
            pl.BlockSpec((1, gla_norm.shape[1]), lambda i: (0, 0)),
            pl.BlockSpec(w_a.shape, lambda i: (0, 0), **resident),
            pl.BlockSpec(w_b.shape, lambda i: (0, 0), **resident),
            pl.BlockSpec((tm, d), lambda i: (i, gate_col // d)),
            pl.BlockSpec((tm, d), lambda i: (i, gate_col // d + 1)),
            pl.BlockSpec((2, d), lambda i: (0, 0)),
            pl.BlockSpec(w_o.shape, lambda i: (0, 0), **resident),
            pl.BlockSpec((tm, d), lambda i: (i, 0)),
            pl.BlockSpec((1, d), lambda i: (0, 0)),
        ],
        out_specs=pl.BlockSpec((tm, d), lambda i: (i, 0)),
        out_shape=jax.ShapeDtypeStruct((n, d), F32),
        compiler_params=_params(("parallel",)),
        name="mix_out",
    )(pm, o, z, gla_norm, w_a, w_b, z, z, bias, w_o, x2, g_post)


def _ffn_kernel(x_ref, gpre_ref, wg_ref, wu_ref, wd_ref, gpost_ref, o_ref, h_ref, acc_ref, *, splits):
    j = pl.program_id(1)
    last = pl.num_programs(1) - 1
    rb = x_ref.shape[0] // splits
    blocks = [slice(r0, r0 + rb) for r0 in range(0, x_ref.shape[0], rb)]

    def hidden_tile(h):
        gate = _dot(h, wg_ref[...])
        up = _dot(h, wu_ref[...])
        act = (gate * jax.nn.sigmoid(gate) * up).astype(BF16)
        return _dot(act, wd_ref[...])

    @pl.when(j == 0)
    def _():
        for rows in blocks:
            h = _rms(x_ref[rows, :], gpre_ref[...]).astype(BF16)
            h_ref[rows, :] = h
            acc_ref[rows, :] = hidden_tile(h)

    @pl.when((j > 0) & (j < last))
    def _():
        acc_ref[...] += hidden_tile(h_ref[...])

    @pl.when(j == last)
    def _():
        for rows in blocks:
            f = acc_ref[rows, :] + hidden_tile(h_ref[rows, :])
            o_ref[rows, :] = x_ref[rows, :] + _rms(f, gpost_ref[...])


def _ffn(x1, g_pre, w_gate, w_up, w_down, g_post, tm, tf):
    n, d = x1.shape
    dff = w_gate.shape[1]
    nf = dff // tf
    ft = lambda i, j: jnp.where(i % 2 == 0, j, nf - 1 - j)
    return pl.pallas_call(
        functools.partial(_ffn_kernel, splits=2),
        grid=(n // tm, nf),
        in_specs=[
            pl.BlockSpec((tm, d), lambda i, j: (i, 0)),
            pl.BlockSpec((1, d), lambda i, j: (0, 0)),
            pl.BlockSpec((d, tf), lambda i, j: (0, ft(i, j))),
            pl.BlockSpec((d, tf), lambda i, j: (0, ft(i, j))),
            pl.BlockSpec((tf, d), lambda i, j: (ft(i, j), 0)),
            pl.BlockSpec((1, d), lambda i, j: (0, 0)),
        ],
        out_specs=pl.BlockSpec((tm, d), lambda i, j: (i, 0)),
        out_shape=jax.ShapeDtypeStruct((n, d), F32),
        scratch_shapes=[pltpu.VMEM((tm, d), BF16), pltpu.VMEM((tm, d), F32)],
        compiler_params=_params(("parallel", "arbitrary")),
        name="ffn",
    )(x1, g_pre, w_gate, w_up, w_down, g_post)


def _layer(x2, batch, seq, norm_mix_pre, w_in, w_gate_up, b_gate, w_pool, pool_scale, gla_norm,
           w_branch_a, w_branch_b, b_branch_gates, w_out, norm_mix_post,
           norm_ffn_pre, w_ffn_gate, w_ffn_up, w_ffn_down, norm_ffn_post):
    pool_w = w_branch_a.shape[0]
    key_w = w_gate_up.shape[1]
    val_w = w_branch_b.shape[0]
    rank = w_gate_up.shape[0]

    o_v = pool_w + 2 * key_w
    o_g = o_v + val_w
    o_r = o_g + rank
    tn_lo = 1024
    w_in_b = w_in.astype(BF16)
    w_hi = w_in_b[:, o_r:]
    n_lo = o_g // tn_lo
    n_front = o_v // tn_lo
    c_v, c_p = 0, val_w
    c_q, c_k = c_p + pool_w, c_p + pool_w + key_w
    lo_out_tile = lambda s: jnp.where(s < n_front, s + val_w // tn_lo, s - n_front)
    c_r, c_gate = 0, val_w
    w_glr = jnp.pad(w_in_b[:, o_g:o_r], ((0, 0), (0, LANES - rank)))
    w_gu = jnp.pad(w_gate_up, ((0, LANES - rank), (0, 0))).astype(BF16)

    z_lo, z_hi, glr = _inproj(x2, norm_mix_pre[None, :], w_in_b, w_hi, w_glr, n_lo, lo_out_tile,
                              tm=1024, tn_lo=tn_lo, tn_hi=2048)
    pm = _pool(z_lo, w_pool.astype(BF16), pool_scale[None, :], batch, seq, c_p // pool_w, tt=2048)
    o = _gla(z_lo, glr, w_gu, b_gate[None, :], val_w, batch, seq, (c_q, c_k, c_v), tt=512)
    x1 = _mix_out(pm, o, z_hi, gla_norm[None, :], w_branch_a.astype(BF16), w_branch_b.astype(BF16),
                  b_branch_gates, w_out.astype(BF16), x2, norm_mix_post[None, :], c_r, c_gate, tm=512)
    return _ffn(x1, norm_ffn_pre[None, :], w_ffn_gate.astype(BF16), w_ffn_up.astype(BF16),
                w_ffn_down.astype(BF16), norm_ffn_post[None, :], tm=1024, tf=512)


def kernel(x, norm_mix_pre, w_in, w_gate_up, b_gate, w_pool, pool_scale, gla_norm, w_branch_a,
           w_branch_b, b_branch_gates, w_out, norm_mix_post, norm_ffn_pre, w_ffn_gate, w_ffn_up,
           w_ffn_down, norm_ffn_post):
    batch, seq, d = x.shape
    x2 = x.reshape(batch * seq, d)
    params = (norm_mix_pre, w_in, w_gate_up, b_gate, w_pool, pool_scale, gla_norm, w_branch_a,
              w_branch_b, b_branch_gates, w_out, norm_mix_post, norm_ffn_pre, w_ffn_gate,
              w_ffn_up, w_ffn_down, norm_ffn_post)
    for layer in range(norm_mix_pre.shape[0]):
        x2 = _layer(x2, batch, seq, *(p[layer] for p in params))
    return x2.reshape(batch, seq, d)
```
